```python
import jax, jax.numpy as jnp
from jax import lax
import numpy as np

D_MODEL = 1024
BATCH = 16
SEQ = 2048
DEPTH = 2
DEC_BATCH = 128
DEC_SEQ = 4
PAST_LEN = 16384
PAGE_SIZE = 128

D_MIX = D_MODEL
MLA_HEADS = 4
MLA_NOPE = 128
MLA_ROPE = 64
MLA_V = 128
MLA_WIDTH = MLA_HEADS * MLA_V
Q_RANK = D_MODEL // 4
KV_RANK = D_MODEL // 4
ROPE_BASE = 10000.0
ATTN_BLOCK = 128
POOL_WINDOWS = (2, 4, 8, 16)
POOL_GROUPS = 4
POOL_WIDTH = D_MIX // 4
POOL_GC = POOL_WIDTH // POOL_GROUPS
POOL_BUF = 15
GLA_HEADS = 4
GLA_WIDTH = D_MIX // 4
GLA_DK = GLA_WIDTH // 2 // GLA_HEADS
GLA_DV = GLA_WIDTH // GLA_HEADS
GLA_GATE_RANK = 16
GLA_TAU = 16.0
GLA_CHUNK = 64
NORM_EPS = 1e-6

IN_SPLITS = (
    ("cq", Q_RANK), ("ckv", KV_RANK), ("kr", MLA_ROPE), ("g_mla", MLA_WIDTH),
    ("pool", POOL_WIDTH), ("g_pool", POOL_WIDTH),
    ("q_gla", GLA_HEADS * GLA_DK), ("k_gla", GLA_HEADS * GLA_DK), ("v_gla", GLA_WIDTH),
    ("a_gla", GLA_GATE_RANK), ("g_gla", GLA_WIDTH),
)
D_IN = 2384

kernel_name = "hybrid_mla_pool_gla_decode_step"


def _rms_norm(x, g):
    xf = x.astype(jnp.float32)
    y = xf * lax.rsqrt(jnp.mean(xf * xf, axis=-1, keepdims=True) + NORM_EPS)
    return (y * g.astype(jnp.float32)).astype(x.dtype)


def _split_in(p):
    parts, off = {}, 0
    for name, n in IN_SPLITS:
        parts[name] = p[..., off:off + n]
        off += n
    return parts


def _rope_tables(pos):
    half = MLA_ROPE // 2
    inv = 1.0 / (ROPE_BASE ** (jnp.arange(half, dtype=jnp.float32) / half))
    ang = pos.astype(jnp.float32)[:, None] * inv[None, :]
    return jnp.cos(ang), jnp.sin(ang)


def _rope(x, cos, sin):
    half = MLA_ROPE // 2
    xf = x.astype(jnp.float32)
    x1, x2 = xf[..., :half], xf[..., half:]
    return jnp.concatenate([x1 * cos - x2 * sin, x2 * cos + x1 * sin], axis=-1).astype(x.dtype)


def _mla_attend_prompt(q_lat, q_rope, ckv, krope):
    B, T, H, R = q_lat.shape
    nb = T // ATTN_BLOCK
    scale = (MLA_NOPE + MLA_ROPE) ** -0.5
    kpos = jnp.arange(T)
    ql = jnp.moveaxis(q_lat.reshape(B, nb, ATTN_BLOCK, H, R), 1, 0)
    qr = jnp.moveaxis(q_rope.reshape(B, nb, ATTN_BLOCK, H, MLA_ROPE), 1, 0)

    def block(args):
        ql_b, qr_b, i = args
        s = (jnp.einsum("bqhr,bkr->bhqk", ql_b, ckv)
             + jnp.einsum("bqhd,bkd->bhqk", qr_b, krope)).astype(jnp.float32) * scale
        qpos = i * ATTN_BLOCK + jnp.arange(ATTN_BLOCK)
        s = jnp.where(kpos[None, :] <= qpos[:, None], s, -jnp.inf)
        p = jax.nn.softmax(s, axis=-1).astype(ckv.dtype)
        return jnp.einsum("bhqk,bkr->bqhr", p, ckv)

    o = lax.map(block, (ql, qr, jnp.arange(nb)))
    return jnp.moveaxis(o, 0, 1).reshape(B, T, H, R)


def _mla_attend_sample(q_lat, q_rope, ckv, krope, ckv_past, krope_past):
    T = q_lat.shape[1]
    L = ckv_past.shape[1]
    scale = (MLA_NOPE + MLA_ROPE) ** -0.5
    s_past = (jnp.einsum("bqhr,bkr->bhqk", q_lat, ckv_past)
              + jnp.einsum("bqhd,bkd->bhqk", q_rope, krope_past)).astype(jnp.float32) * scale
    s_new = (jnp.einsum("bqhr,bkr->bhqk", q_lat, ckv)
             + jnp.einsum("bqhd,bkd->bhqk", q_rope, krope)).astype(jnp.float32) * scale
    causal = jnp.arange(T)[None, :] <= jnp.arange(T)[:, None]
    s_new = jnp.where(causal, s_new, -jnp.inf)
    p = jax.nn.softmax(jnp.concatenate([s_past, s_new], axis=-1), axis=-1).astype(ckv.dtype)
    return (jnp.einsum("bhqk,bkr->bqhr", p[..., :L], ckv_past)
            + jnp.einsum("bhqk,bkr->bqhr", p[..., L:], ckv))


def _pool_mix(xp, prefix, past_len, w_pool, pool_scale):
    B, T, C = xp.shape
    z = jnp.concatenate([prefix.astype(xp.dtype), xp], axis=1).astype(jnp.float32)
    cs = jnp.concatenate([jnp.zeros((B, 1, C), jnp.float32), jnp.cumsum(z, axis=1)], axis=1)
    end = cs[:, POOL_BUF + 1:POOL_BUF + 1 + T]
    t = jnp.arange(T)
    means = []
    for g, w in enumerate(POOL_WINDOWS):
        sl = slice(g * POOL_GC, (g + 1) * POOL_GC)
        win_sum = end[..., sl] - cs[:, POOL_BUF + 1 - w:POOL_BUF + 1 - w + T, sl]
        cnt = jnp.minimum(past_len + t + 1, w).astype(jnp.float32)
        means.append(win_sum / cnt[None, :, None])
    pooled = jnp.concatenate(means, axis=-1) - xp.astype(jnp.float32)
    pooled = pooled.reshape(B, T, POOL_GROUPS, POOL_GC)
    y = jnp.einsum("btgc,gcd->btgd", pooled, w_pool.astype(jnp.float32)).reshape(B, T, C)
    y = y * pool_scale.astype(jnp.float32)
    return y.astype(xp.dtype), z[:, -POOL_BUF:].astype(xp.dtype)


def _gla(q, k, v, log_a, s0):
    dt = q.dtype
    B, T, H, _ = q.shape
    C = GLA_CHUNK if T >= GLA_CHUNK else T
    pad = (-T) % C
    n = (T + pad) // C

    def prep(a):
        a = jnp.pad(a.astype(jnp.float32), ((0, 0), (0, pad), (0, 0), (0, 0)))
        return a.reshape(B, n, C, H, a.shape[-1])

    q = prep(q) * (GLA_DK ** -0.5)
    k, v, la = prep(k), prep(v), prep(log_a)
    b = jnp.cumsum(la, axis=2)
    b_last = b[:, :, -1]
    qt = q * jnp.exp(b)
    kt = k * jnp.exp(-b)
    kd = k * jnp.exp(b_last[:, :, None] - b)
    mask = jnp.arange(C)[:, None] >= jnp.arange(C)[None, :]
    A = jnp.where(mask, jnp.einsum("bnchd,bnshd->bnhcs", qt, kt), 0.0)
    o_intra = jnp.einsum("bnhcs,bnshv->bnchv", A, v)
    kv = jnp.einsum("bnshd,bnshv->bnhdv", kd, v)

    def step(S, inp):
        qt_c, dec_c, kv_c = inp
        o = jnp.einsum("bchd,bhdv->bchv", qt_c, S)
        return dec_c[..., None] * S + kv_c, o

    S, o_inter = lax.scan(step, s0.astype(jnp.float32),
                          (jnp.moveaxis(qt, 1, 0), jnp.moveaxis(jnp.exp(b_last), 1, 0),
                           jnp.moveaxis(kv, 1, 0)))
    o = (o_intra + jnp.moveaxis(o_inter, 0, 1)).reshape(B, T + pad, H, GLA_DV)[:, :T]
    return o.astype(dt), S.astype(dt)


def _layer(h, cos, sin, past_len, ckv_past, krope_past, pool_prefix, gla_state,
           g_pre, w_in, g_cq, w_uq, g_ckv, w_uk, w_uv, w_pool, pool_scale,
           w_a2, b_a, g_gla, w_out, g_post):
    B, T, _ = h.shape
    u = _rms_norm(h, g_pre)
    p = _split_in(u @ w_in)
    cq = _rms_norm(p["cq"], g_cq)
    q = jnp.einsum("btr,rhd->bthd", cq, w_uq)
    q_nope = q[..., :MLA_NOPE]
    q_rope = _rope(q[..., MLA_NOPE:], cos[:, None, :], sin[:, None, :])
    ckv = _rms_norm(p["ckv"], g_ckv)
    krope = _rope(p["kr"], cos, sin)
    q_lat = jnp.einsum("bthn,rhn->bthr", q_nope, w_uk)
    if ckv_past is None:
        o_lat = _mla_attend_prompt(q_lat, q_rope, ckv, krope)
    else:
        o_lat = _mla_attend_sample(q_lat, q_rope, ckv, krope, ckv_past, krope_past)
    o_mla = jnp.einsum("bthr,rhv->bthv", o_lat, w_uv).reshape(B, T, MLA_WIDTH)
    o_pool, pool_new = _pool_mix(p["pool"], pool_prefix, past_len, w_pool, pool_scale)
    qg = p["q_gla"].reshape(B, T, GLA_HEADS, GLA_DK)
    kg = p["k_gla"].reshape(B, T, GLA_HEADS, GLA_DK)
    vg = p["v_gla"].reshape(B, T, GLA_HEADS, GLA_DV)
    log_a = jax.nn.log_sigmoid((p["a_gla"] @ w_a2 + b_a).astype(jnp.float32)) / GLA_TAU
    o_gla, gla_new = _gla(qg, kg, vg, log_a.reshape(B, T, GLA_HEADS, GLA_DK), gla_state)
    o_gla = _rms_norm(o_gla, g_gla.reshape(GLA_HEADS, GLA_DV)).reshape(B, T, GLA_WIDTH)
    mixed = jnp.concatenate([o_mla * jax.nn.silu(p["g_mla"]),
                             o_pool * jax.nn.silu(p["g_pool"]),
                             o_gla * jax.nn.silu(p["g_gla"])], axis=-1)
    h = h + _rms_norm(mixed @ w_out, g_post)
    return h, ckv, krope, pool_new, gla_new


def setup_inputs(seed: int = 0) -> dict:
    key = jax.random.key(seed)
    ks = jax.random.split(key, 24)
    n_pages = PAST_LEN // PAGE_SIZE
    n_used = DEC_BATCH * n_pages
    n_pool = n_used + n_used // 4
    f32 = jnp.float32
    nrm = lambda k, s, sc=1.0: jax.random.normal(k, s, f32) * sc
    gain = lambda k, s: 1.0 + 0.1 * jax.random.normal(k, s, f32)
    page_table = jax.random.permutation(ks[6], n_pool)[:n_used].reshape(DEC_BATCH, n_pages).astype(jnp.int32)
    return {
        "x_prompt": nrm(ks[0], (BATCH, SEQ, D_MODEL)),
        "x_sample": nrm(ks[1], (DEC_BATCH, DEC_SEQ, D_MODEL)),
        "cache_ckv": nrm(ks[2], (DEPTH, n_pool, PAGE_SIZE, KV_RANK)),
        "cache_krope": nrm(ks[3], (DEPTH, n_pool, PAGE_SIZE, MLA_ROPE)),
        "state_pool": nrm(ks[4], (DEPTH, DEC_BATCH, POOL_BUF, POOL_WIDTH)),
        "state_gla": nrm(ks[5], (DEPTH, DEC_BATCH, GLA_HEADS, GLA_DK, GLA_DV)),
        "page_table": page_table,
        "g_pre": gain(ks[7], (DEPTH, D_MODEL)),
        "w_in": nrm(ks[8], (DEPTH, D_MODEL, D_IN), D_MODEL ** -0.5),
        "g_cq": gain(ks[9], (DEPTH, Q_RANK)),
        "w_uq": nrm(ks[10], (DEPTH, Q_RANK, MLA_HEADS, MLA_NOPE + MLA_ROPE), Q_RANK ** -0.5),
        "g_ckv": gain(ks[11], (DEPTH, KV_RANK)),
        "w_uk": nrm(ks[12], (DEPTH, KV_RANK, MLA_HEADS, MLA_NOPE), KV_RANK ** -0.5),
        "w_uv": nrm(ks[13], (DEPTH, KV_RANK, MLA_HEADS, MLA_V), KV_RANK ** -0.5),
        "w_pool": nrm(ks[14], (DEPTH, POOL_GROUPS, POOL_GC, POOL_GC), POOL_GC ** -0.5),
        "pool_scale": gain(ks[15], (DEPTH, POOL_WIDTH)),
        "w_a2": nrm(ks[16], (DEPTH, GLA_GATE_RANK, GLA_HEADS * GLA_DK), GLA_GATE_RANK ** -0.5),
        "b_a": nrm(ks[17], (DEPTH, GLA_HEADS * GLA_DK), 0.1),
        "g_gla": gain(ks[18], (DEPTH, GLA_WIDTH)),
        "w_out": nrm(ks[19], (DEPTH, D_MIX, D_MODEL), D_MIX ** -0.5),
        "g_post": gain(ks[20], (DEPTH, D_MODEL)),
    }


def reference(x_prompt, x_sample, cache_ckv, cache_krope, state_pool, state_gla, page_table,
              g_pre, w_in, g_cq, w_uq, g_ckv, w_uk, w_uv, w_pool, pool_scale,
              w_a2, b_a, g_gla, w_out, g_post):
    Bp, T_p, _ = x_prompt.shape
    Bs, T_s, _ = x_sample.shape
    n_pages = page_table.shape[1]
    past_len = n_pages * PAGE_SIZE
    cos_p, sin_p = _rope_tables(jnp.arange(T_p))
    cos_s, sin_s = _rope_tables(past_len + jnp.arange(T_s))
    pool_zero = jnp.zeros((Bp, POOL_BUF, POOL_WIDTH), x_prompt.dtype)
    gla_zero = jnp.zeros((Bp, GLA_HEADS, GLA_DK, GLA_DV), x_prompt.dtype)
    hp, hs = x_prompt, x_sample
    ckv_p, kr_p, pool_p, gla_p = [], [], [], []
    ckv_s, kr_s, pool_s, gla_s = [], [], [], []
    for l in range(DEPTH):
        hp, a, b, c, d = _layer(hp, cos_p, sin_p, 0, None, None, pool_zero, gla_zero,
                                g_pre[l], w_in[l], g_cq[l], w_uq[l], g_ckv[l], w_uk[l], w_uv[l],
                                w_pool[l], pool_scale[l], w_a2[l], b_a[l], g_gla[l], w_out[l], g_post[l])
        ckv_p.append(a); kr_p.append(b); pool_p.append(c); gla_p.append(d)
        ckv_past = cache_ckv[l, page_table].reshape(Bs, past_len, KV_RANK)
        kr_past = cache_krope[l, page_table].reshape(Bs, past_len, MLA_ROPE)
        hs, a, b, c, d = _layer(hs, cos_s, sin_s, past_len, ckv_past, kr_past, state_pool[l], state_gla[l],
                                g_pre[l], w_in[l], g_cq[l], w_uq[l], g_ckv[l], w_uk[l], w_uv[l],
                                w_pool[l], pool_scale[l], w_a2[l], b_a[l], g_gla[l], w_out[l], g_post[l])
        ckv_s.append(a); kr_s.append(b); pool_s.append(c); gla_s.append(d)
    return (hp, hs,
            jnp.stack(ckv_p), jnp.stack(kr_p), jnp.stack(pool_p), jnp.stack(gla_p),
            jnp.stack(ckv_s), jnp.stack(kr_s), jnp.stack(pool_s), jnp.stack(gla_s))
```

```python
import functools

import jax
import jax.numpy as jnp
from jax import lax
from jax.experimental import pallas as pl
from jax.experimental.pallas import tpu as pltpu

F32 = jnp.float32
BF16 = jnp.bfloat16

D_MODEL = 1024
MLA_HEADS = 4
MLA_NOPE = 128
MLA_ROPE = 64
MLA_V = 128
Q_RANK = 256
KV_RANK = 256
ROPE_BASE = 10000.0
POOL_WINDOWS = (2, 4, 8, 16)
POOL_WIDTH = 256
POOL_GC = 64
POOL_BUF = 15
GLA_HEADS = 4
GLA_DK = 32
GLA_DV = 64
GLA_WIDTH = 256
GLA_GATE_RANK = 16
GLA_TAU = 16.0
GLA_CHUNK = 64
NORM_EPS = 1e-6
PAGE_SIZE = 128
ATTN_SCALE = (MLA_NOPE + MLA_ROPE) ** -0.5
GLA_SCALE = GLA_DK ** -0.5

LANES = 128
VMEM_LIMIT_BYTES = 56 * 1024 * 1024

C_CQ = 0
C_CKV = 256
C_KR = 512
C_REST = 640
C_A = 2432
D_EXT = 2560
R_GMLA = 0
R_POOL = 512
R_GPOOL = 768
R_Q = 1024
R_K = 1152
R_V = 1280
R_GGLA = 1536
R_LOGA = 1792
D_REST = 1920

NEG_BIG = -1e30


def _dot(a, b):
    return jnp.dot(a, b, preferred_element_type=F32)


def _dot_nt(a, b):
    return lax.dot_general(a, b, (((1,), (1,)), ((), ())), preferred_element_type=F32)


def _dot_tn(a, b):
    return lax.dot_general(a, b, (((0,), (0,)), ((), ())), preferred_element_type=F32)


def _rms(x, g):
    return x * lax.rsqrt(jnp.mean(x * x, axis=-1, keepdims=True) + NORM_EPS) * g


def _silu(x):
    return x / (1.0 + jnp.exp(-x))


def _split3(x):
    hi = x.astype(BF16)
    r = x - hi.astype(F32)
    mid = r.astype(BF16)
    lo = (r - mid.astype(F32)).astype(BF16)
    return hi, mid, lo


def _dot3(mat, parts):
    return _dot(mat, parts[0]) + _dot(mat, parts[1]) + _dot(mat, parts[2])


def _iota(shape, dim):
    return lax.broadcasted_iota(jnp.int32, shape, dim)


def _inproj_body(x_ref, tab_ref, gpre_ref, win_ref, gcq_ref, wuq_ref, gckv_ref, wuk_ref,
                 wa2_ref, ba_ref,
                 ql_ref, qr_ref, ckv_ref, kr_ref, ckvb_ref, krb_ref, rest_ref):
    u = _rms(x_ref[...], gpre_ref[...])
    p = _dot(u.astype(BF16), win_ref[...])
    tab = tab_ref[...]
    low_half = (_iota((1, LANES), 1) < MLA_ROPE).astype(F32)

    def rope(slab):
        t = slab * tab
        return t + pltpu.roll(t, MLA_ROPE, axis=1)

    cq = _rms(p[:, C_CQ:C_CQ + Q_RANK], gcq_ref[...])
    q = _dot(cq.astype(BF16), wuq_ref[...])
    for h in range(MLA_HEADS):
        base = h * 256
        q_nope = q[:, base:base + MLA_NOPE]
        q_lat = _dot(q_nope.astype(BF16), wuk_ref[h]) * ATTN_SCALE
        ql_ref[h] = q_lat.astype(BF16)
        q_rope = rope(q[:, base + MLA_NOPE:base + 256]) * (low_half * ATTN_SCALE)
        qr_ref[h] = q_rope.astype(BF16)

    ckv = _rms(p[:, C_CKV:C_CKV + KV_RANK], gckv_ref[...])
    ckv_ref[...] = ckv
    ckvb_ref[...] = ckv.astype(BF16)
    kr = rope(p[:, C_KR:C_KR + LANES])
    kr_ref[...] = kr[:, :MLA_ROPE]
    krb_ref[...] = (kr * low_half).astype(BF16)

    rest_ref[:, 0:R_LOGA] = p[:, C_REST:C_A]
    z = _dot(p[:, C_A:D_EXT].astype(BF16), wa2_ref[...]) + ba_ref[...]
    log_sig = jnp.minimum(z, 0.0) - jnp.log(1.0 + jnp.exp(-jnp.abs(z)))
    rest_ref[:, R_LOGA:D_REST] = log_sig * (1.0 / GLA_TAU)


def _inproj(x, tab, w, *, tm):
    n = x.shape[0]
    nt = tab.shape[0] // tm
    full = lambda a: pl.BlockSpec(a.shape, lambda i, nd=a.ndim: (0,) * nd)
    row = lambda width: pl.BlockSpec((tm, width), lambda i: (i, 0))
    head = lambda width: pl.BlockSpec((MLA_HEADS, tm, width), lambda i: (0, i, 0))
    weights = (w["g_pre"], w["w_in"], w["g_cq"], w["w_uq"], w["g_ckv"], w["w_uk"], w["w_a2"], w["b_a"])
    return pl.pallas_call(
        _inproj_body,
        grid=(n // tm,),
        in_specs=[row(D_MODEL), pl.BlockSpec((tm, LANES), lambda i: (i % nt, 0))]
        + [full(a) for a in weights],
        out_specs=[head(KV_RANK), head(LANES), row(KV_RANK), row(MLA_ROPE), row(KV_RANK), row(LANES),
                   row(D_REST)],
        out_shape=[
            jax.ShapeDtypeStruct((MLA_HEADS, n, KV_RANK), BF16),
            jax.ShapeDtypeStruct((MLA_HEADS, n, LANES), BF16),
            jax.ShapeDtypeStruct((n, KV_RANK), F32),
            jax.ShapeDtypeStruct((n, MLA_ROPE), F32),
            jax.ShapeDtypeStruct((n, KV_RANK), BF16),
            jax.ShapeDtypeStruct((n, LANES), BF16),
            jax.ShapeDtypeStruct((n, D_REST), F32),
        ],
        compiler_params=pltpu.CompilerParams(dimension_semantics=("parallel",),
                                             vmem_limit_bytes=VMEM_LIMIT_BYTES),
        name="inproj",
    )(x, tab, *weights)


def _attn_prompt_body(ql_ref, qr_ref, k_ref, kr_ref, wuv_ref, o_ref, m_ref, l_ref, acc_ref, *, tq):
    i = pl.program_id(1)
    rows = MLA_HEADS * tq
    ql = ql_ref[...].reshape(rows, KV_RANK)
    qr = qr_ref[...].reshape(rows, LANES)
    m_ref[...] = jnp.full((rows, 1), NEG_BIG, F32)
    l_ref[...] = jnp.zeros((rows, 1), F32)
    acc_ref[...] = jnp.zeros((rows, KV_RANK), F32)

    def step(j, diagonal):
        start = pl.multiple_of(j * tq, tq)
        kb = k_ref[pl.ds(start, tq), :]
        krb = kr_ref[pl.ds(start, tq), :]
        s = _dot_nt(ql, kb) + _dot_nt(qr, krb)
        if diagonal:
            qpos = _iota((rows, tq), 0) & (tq - 1)
            kpos = _iota((rows, tq), 1)
            s = jnp.where(kpos <= qpos, s, NEG_BIG)
        m_old = m_ref[...]
        m_new = jnp.maximum(m_old, jnp.max(s, axis=1, keepdims=True))
        alpha = jnp.exp(m_old - m_new)
        p = jnp.exp(s - m_new)
        l_ref[...] = alpha * l_ref[...] + jnp.sum(p, axis=1, keepdims=True)
        acc_ref[...] = alpha * acc_ref[...] + _dot(p.astype(BF16), kb)
        m_ref[...] = m_new

    def body(j, carry):
        step(j, False)
        return carry

    lax.fori_loop(0, i, body, 0)
    step(i, True)

    o = acc_ref[...] * (1.0 / l_ref[...])
    for h in range(MLA_HEADS):
        o_h = o[h * tq:(h + 1) * tq].astype(BF16)
        o_ref[:, h * MLA_V:(h + 1) * MLA_V] = _dot(o_h, wuv_ref[h])


def _attn_prompt(ql, qr, ckvb, krb, wuv, *, batch, seq, tq):
    n = batch * seq
    nq = seq // tq
    rows = MLA_HEADS * tq
    return pl.pallas_call(
        functools.partial(_attn_prompt_body, tq=tq),
        grid=(batch, nq),
        in_specs=[
            pl.BlockSpec((MLA_HEADS, tq, KV_RANK), lambda b, i: (0, b * nq + i, 0)),
            pl.BlockSpec((MLA_HEADS, tq, LANES), lambda b, i: (0, b * nq + i, 0)),
            pl.BlockSpec((seq, KV_RANK), lambda b, i: (b, 0)),
            pl.BlockSpec((seq, LANES), lambda b, i: (b, 0)),
            pl.BlockSpec(wuv.shape, lambda b, i: (0, 0, 0)),
        ],
        out_specs=pl.BlockSpec((tq, MLA_HEADS * MLA_V), lambda b, i: (b * nq + i, 0)),
        out_shape=jax.ShapeDtypeStruct((n, MLA_HEADS * MLA_V), F32),
        scratch_shapes=[pltpu.VMEM((rows, 1), F32), pltpu.VMEM((rows, 1), F32),
                        pltpu.VMEM((rows, KV_RANK), F32)],
        compiler_params=pltpu.CompilerParams(dimension_semantics=("parallel", "arbitrary"),
                                             vmem_limit_bytes=VMEM_LIMIT_BYTES),
        name="attn_prompt",
    )(ql, qr, ckvb, krb, wuv)


def _attn_sample_body(pt_ref, ql_ref, qr_ref, cn_ref, krn_ref, *refs, pages, t_new):
    del pt_ref
    ckv_pages = refs[:pages]
    kr_pages = refs[pages:2 * pages]
    o_ref = refs[2 * pages]
    kbuf, rbuf, m_ref, l_ref, acc_ref = refs[2 * pages + 1:]
    c = pl.program_id(1)
    rows = MLA_HEADS * t_new
    ql = ql_ref[0]
    qr = qr_ref[0, :, 0:MLA_ROPE]

    @pl.when(c == 0)
    def _():
        qlf = ql.astype(F32)
        qrf = qr.astype(F32)
        cn = cn_ref[0]
        krn = krn_ref[0]
        t_row = _iota((rows, 1), 0) & (t_new - 1)
        scores = []
        for t2 in range(t_new):
            sc = (jnp.sum(qlf * cn[t2:t2 + 1, :], axis=1, keepdims=True)
                  + jnp.sum(qrf * krn[t2:t2 + 1, :], axis=1, keepdims=True))
            scores.append(jnp.where(t2 <= t_row, sc, NEG_BIG))
        m = scores[0]
        for sc in scores[1:]:
            m = jnp.maximum(m, sc)
        l = jnp.zeros((rows, 1), F32)
        acc = jnp.zeros((rows, KV_RANK), F32)
        for t2 in range(t_new):
            p = jnp.exp(scores[t2] - m)
            l = l + p
            acc = acc + p * cn[t2:t2 + 1, :]
        m_ref[...] = m
        l_ref[...] = l
        acc_ref[...] = acc

    for i in range(pages):
        kbuf[i * PAGE_SIZE:(i + 1) * PAGE_SIZE, :] = ckv_pages[i][...].astype(BF16)
        rbuf[i * PAGE_SIZE:(i + 1) * PAGE_SIZE, :] = kr_pages[i][...].astype(BF16)
    kb = kbuf[...]
    s = _dot_nt(ql, kb) + _dot_nt(qr, rbuf[...])
    m_old = m_ref[...]
    m_new = jnp.maximum(m_old, jnp.max(s, axis=1, keepdims=True))
    alpha = jnp.exp(m_old - m_new)
    p = jnp.exp(s - m_new)
    l_ref[...] = alpha * l_ref[...] + jnp.sum(p, axis=1, keepdims=True)
    acc_ref[...] = alpha * acc_ref[...] + _dot(p.astype(BF16), kb)
    m_ref[...] = m_new

    @pl.when(c == pl.num_programs(1) - 1)
    def _():
        o_ref[0] = acc_ref[...] * (1.0 / l_ref[...])


def _attn_sample(page_table_flat, ql, qr, ckv_new, kr_new, cache_ckv, cache_krope, *, layer, n_pages,
                 pages):
    bs, rows, _ = ql.shape
    t_new = ckv_new.shape[1]
    n_chunks = n_pages // pages
    page_spec = lambda width, i: pl.BlockSpec(
        (None, None, PAGE_SIZE, width),
        lambda b, c, pt: (layer, pt[b * n_pages + c * pages + i], 0, 0))
    grid_spec = pltpu.PrefetchScalarGridSpec(
        num_scalar_prefetch=1,
        grid=(bs, n_chunks),
        in_specs=[
            pl.BlockSpec((1, rows, KV_RANK), lambda b, c, pt: (b, 0, 0)),
            pl.BlockSpec((1, rows, LANES), lambda b, c, pt: (b, 0, 0)),
            pl.BlockSpec((1, t_new, KV_RANK), lambda b, c, pt: (b, 0, 0)),
            pl.BlockSpec((1, t_new, MLA_ROPE), lambda b, c, pt: (b, 0, 0)),
        ] + [page_spec(KV_RANK, i) for i in range(pages)] + [page_spec(MLA_ROPE, i) for i in range(pages)],
        out_specs=pl.BlockSpec((1, rows, KV_RANK), lambda b, c, pt: (b, 0, 0)),
        scratch_shapes=[
            pltpu.VMEM((pages * PAGE_SIZE, KV_RANK), BF16),
            pltpu.VMEM((pages * PAGE_SIZE, MLA_ROPE), BF16),
            pltpu.VMEM((rows, 1), F32), pltpu.VMEM((rows, 1), F32), pltpu.VMEM((rows, KV_RANK), F32),
        ],
    )
    return pl.pallas_call(
        functools.partial(_attn_sample_body, pages=pages, t_new=t_new),
        grid_spec=grid_spec,
        out_shape=jax.ShapeDtypeStruct((bs, rows, KV_RANK), F32),
        compiler_params=pltpu.CompilerParams(dimension_semantics=("parallel", "arbitrary"),
                                             vmem_limit_bytes=VMEM_LIMIT_BYTES),
        name="attn_sample",
    )(page_table_flat, ql, qr, ckv_new, kr_new, *([cache_ckv] * pages), *([cache_krope] * pages))


def _head_rms(o, g):
    o2 = o * o
    hi = o2.astype(BF16)
    lo = (o2 - hi.astype(F32)).astype(BF16)
    same_head = (_iota((GLA_WIDTH, GLA_WIDTH), 0) >> 6) == (_iota((GLA_WIDTH, GLA_WIDTH), 1) >> 6)
    ones_bd = same_head.astype(BF16)
    ms = (_dot(hi, ones_bd) + _dot(lo, ones_bd)) * (1.0 / GLA_DV)
    return o * lax.rsqrt(ms + NORM_EPS) * g


def _pool_select(s2, s4, s8, s16, count, xp):
    group = _iota((1, POOL_WIDTH), 1) >> 6
    win = jnp.where(group == 0, s2, jnp.where(group == 1, s4, jnp.where(group == 2, s8, s16)))
    return win / count - xp


def _gate_and_project(o_mla, rest, o_pool, o_gla, h, wout_ref, gpost):
    mixed = jnp.concatenate([
        o_mla * _silu(rest[:, R_GMLA:R_GMLA + 512]),
        o_pool * _silu(rest[:, R_GPOOL:R_GPOOL + POOL_WIDTH]),
        o_gla * _silu(rest[:, R_GGLA:R_GGLA + GLA_WIDTH]),
    ], axis=1).astype(BF16)
    y = _dot(mixed, wout_ref[...])
    return h + _rms(y, gpost)


def _mix_prompt_body(rest_ref, omla_ref, h_ref, wpool_ref, pscale_ref, ggla_ref, wout_ref, gpost_ref,
                     hout_ref, sout_ref, hist_ref, st_ref, o_ref, *, tm):
    t = pl.program_id(1)
    n_chunks = tm // GLA_CHUNK

    @pl.when(t == 0)
    def _():
        hist_ref[...] = jnp.zeros(hist_ref.shape, F32)
        st_ref[...] = jnp.zeros(st_ref.shape, F32)

    rest = rest_ref[...]

    xp = rest[:, R_POOL:R_POOL + POOL_WIDTH]
    row = _iota((tm, POOL_WIDTH), 0)

    def shifted(cur, slot, k):
        prev = hist_ref[slot]
        hist_ref[slot] = cur
        return jnp.where(row >= k, pltpu.roll(cur, k, axis=0), pltpu.roll(prev, k, axis=0))

    s2 = xp + shifted(xp, 0, 1)
    s4 = s2 + shifted(s2, 1, 2)
    s8 = s4 + shifted(s4, 2, 4)
    s16 = s8 + shifted(s8, 3, 8)
    window = 2 << (_iota((1, POOL_WIDTH), 1) >> 6)
    count = jnp.minimum(t * tm + row + 1, window).astype(F32)
    pooled = _pool_select(s2, s4, s8, s16, count, xp)
    o_pool = _dot(pooled.astype(BF16), wpool_ref[...]) * pscale_ref[...]

    q = rest[:, R_Q:R_Q + 128]
    k = rest[:, R_K:R_K + 128]
    v = rest[:, R_V:R_V + GLA_WIDTH]
    la = rest[:, R_LOGA:R_LOGA + 128]
    r_i = _iota((tm, tm), 0)
    c_i = _iota((tm, tm), 1)
    same_chunk = (r_i >> 6) == (c_i >> 6)
    tri_bd = (same_chunk & (c_i <= r_i)).astype(BF16)
    ones_bd = same_chunk.astype(BF16)
    la_parts = _split3(la)
    b = _dot3(tri_bd, la_parts)
    b_last = _dot3(ones_bd, la_parts)
    qt = q * GLA_SCALE * jnp.exp(b)
    kt = k * jnp.exp(-b)
    kd = k * jnp.exp(b_last - b)
    dec = jnp.exp(b_last)
    head_mask = ((_iota((4 * GLA_CHUNK, 128), 0) >> 6) == (_iota((4 * GLA_CHUNK, 128), 1) >> 5)).astype(F32)
    causal = (_iota((4 * GLA_CHUNK, GLA_CHUNK), 0) & (GLA_CHUNK - 1)) >= _iota((4 * GLA_CHUNK, GLA_CHUNK), 1)
    lane_head = _iota((GLA_CHUNK, GLA_WIDTH), 1) >> 6
    for n in range(n_chunks):
        lo, hi = n * GLA_CHUNK, (n + 1) * GLA_CHUNK
        q_n = qt[lo:hi]
        v_n = v[lo:hi].astype(BF16)
        q_bd = (jnp.concatenate([q_n] * GLA_HEADS, axis=0) * head_mask).astype(BF16)
        a = _dot_nt(q_bd, kt[lo:hi].astype(BF16))
        a = jnp.where(causal, a, 0.0)
        pv = _dot(a.astype(BF16), v_n)
        o_n = jnp.zeros((GLA_CHUNK, GLA_WIDTH), F32)
        for hh in range(GLA_HEADS):
            o_n = o_n + jnp.where(lane_head == hh, pv[hh * GLA_CHUNK:(hh + 1) * GLA_CHUNK], 0.0)
        st = st_ref[...]
        o_n = o_n + _dot_nt(q_n.astype(BF16), st.astype(BF16))
        upd = _dot_tn(v_n, kd[lo:hi].astype(BF16))
        st_ref[...] = dec[lo:lo + 1, :] * st + upd * head_mask
        o_ref[lo:hi, :] = o_n
    o_gla = _head_rms(o_ref[...], ggla_ref[...])

    hout_ref[...] = _gate_and_project(omla_ref[...], rest, o_pool, o_gla, h_ref[...], wout_ref,
                                      gpost_ref[...])

    @pl.when(t == pl.num_programs(1) - 1)
    def _():
        sout_ref[...] = st_ref[...]


def _mix_prompt(rest, omla, h, w, *, batch, seq, tm):
    n = batch * seq
    nt = seq // tm
    full = lambda a: pl.BlockSpec(a.shape, lambda b, t, nd=a.ndim: (0,) * nd)
    row = lambda width: pl.BlockSpec((tm, width), lambda b, t: (b * nt + t, 0))
    weights = (w["w_pool"], w["pool_scale"], w["g_gla"], w["w_out"], w["g_post"])
    return pl.pallas_call(
        functools.partial(_mix_prompt_body, tm=tm),
        grid=(batch, nt),
        in_specs=[row(D_REST), row(512), row(D_MODEL)] + [full(a) for a in weights],
        out_specs=[row(D_MODEL), pl.BlockSpec((None, 4 * GLA_DV, 128), lambda b, t: (b, 0, 0))],
        out_shape=[jax.ShapeDtypeStruct((n, D_MODEL), F32),
                   jax.ShapeDtypeStruct((batch, 4 * GLA_DV, 128), F32)],
        scratch_shapes=[pltpu.VMEM((4, tm, POOL_WIDTH), F32), pltpu.VMEM((4 * GLA_DV, 128), F32),
                        pltpu.VMEM((tm, GLA_WIDTH), F32)],
        compiler_params=pltpu.CompilerParams(dimension_semantics=("parallel", "arbitrary"),
                                             vmem_limit_bytes=VMEM_LIMIT_BYTES),
        name="mix_prompt",
    )(rest, omla, h, *weights)


def _mix_sample_body(rest_ref, olat_ref, prefix_ref, s0_ref, h_ref, wuv_ref, wpool_ref, pscale_ref,
                     ggla_ref, wout_ref, gpost_ref,
                     hout_ref, snew_ref, st_ref, qT_ref, kT_ref, aT_ref, vT_ref, oT_ref, o_ref,
                     *, t_new, past_len):
    bs = LANES
    n_state_blocks = (GLA_HEADS * GLA_DK * GLA_DV) // LANES
    rest = rest_ref[...]

    o_mla = jnp.concatenate([_dot(olat_ref[h].astype(BF16), wuv_ref[h]) for h in range(MLA_HEADS)], axis=1)

    z = [prefix_ref[j] for j in range(POOL_BUF)]
    z += [rest[t * bs:(t + 1) * bs, R_POOL:R_POOL + POOL_WIDTH] for t in range(t_new)]
    pooled = []
    for t in range(t_new):
        e = POOL_BUF + t
        s2 = z[e] + z[e - 1]
        s4 = s2 + z[e - 2] + z[e - 3]
        s8 = s4 + z[e - 4] + z[e - 5] + z[e - 6] + z[e - 7]
        s16 = s8
        for i in range(8, 16):
            s16 = s16 + z[e - i]
        window = 2 << (_iota((1, POOL_WIDTH), 1) >> 6)
        count = jnp.minimum(past_len + t + 1, window).astype(F32)
        pooled.append(_pool_select(s2, s4, s8, s16, count, z[e]))
    pooled = jnp.concatenate(pooled, axis=0)
    o_pool = _dot(pooled.astype(BF16), wpool_ref[...]) * pscale_ref[...]

    for j in range(n_state_blocks):
        st_ref[j * LANES:(j + 1) * LANES, :] = s0_ref[:, j * LANES:(j + 1) * LANES].T
    for t in range(t_new):
        rows = slice(t * bs, (t + 1) * bs)
        qT_ref[t] = (rest[rows, R_Q:R_Q + 128] * GLA_SCALE).T
        kT_ref[t] = rest[rows, R_K:R_K + 128].T
        aT_ref[t] = jnp.exp(rest[rows, R_LOGA:R_LOGA + 128]).T
        vT_ref[t, 0:LANES] = rest[rows, R_V:R_V + LANES].T
        vT_ref[t, LANES:2 * LANES] = rest[rows, R_V + LANES:R_V + 2 * LANES].T
    for hh in range(GLA_HEADS):
        v_blocks = [vT_ref[t, hh * GLA_DV:(hh + 1) * GLA_DV, :] for t in range(t_new)]

        def body(d, carry, hh=hh, v_blocks=v_blocks):
            r = hh * GLA_DK + d
            base = pl.multiple_of(r * GLA_DV, GLA_DV)
            s = st_ref[pl.ds(base, GLA_DV), :]
            outs = []
            for t in range(t_new):
                s = aT_ref[t, pl.ds(r, 1), :] * s + kT_ref[t, pl.ds(r, 1), :] * v_blocks[t]
                outs.append(carry[t] + qT_ref[t, pl.ds(r, 1), :] * s)
            st_ref[pl.ds(base, GLA_DV), :] = s
            return tuple(outs)

        acc = lax.fori_loop(0, GLA_DK, body, tuple(jnp.zeros((GLA_DV, LANES), F32) for _ in range(t_new)))
        for t in range(t_new):
            oT_ref[t, hh * GLA_DV:(hh + 1) * GLA_DV, :] = acc[t]
    for j in range(n_state_blocks):
        snew_ref[:, j * LANES:(j + 1) * LANES] = st_ref[j * LANES:(j + 1) * LANES, :].T
    for t in range(t_new):
        o_ref[t * bs:(t + 1) * bs, 0:LANES] = oT_ref[t, 0:LANES, :].T
        o_ref[t * bs:(t + 1) * bs, LANES:2 * LANES] = oT_ref[t, LANES:2 * LANES, :].T
    o_gla = _head_rms(o_ref[...], ggla_ref[...])

    hout_ref[...] = _gate_and_project(o_mla, rest, o_pool, o_gla, h_ref[...], wout_ref, gpost_ref[...])


def _mix_sample(rest, olat, prefix_t, s0, h, w, *, t_new, past_len):
    n = rest.shape[0]
    bs = n // t_new
    assert bs == LANES, "sample mixing kernel keeps the sequences on the lane axis"
    state_w = GLA_HEADS * GLA_DK * GLA_DV
    args = (rest, olat, prefix_t, s0, h, w["w_uv"], w["w_pool"], w["pool_scale"], w["g_gla"], w["w_out"],
            w["g_post"])
    full = lambda a: pl.BlockSpec(a.shape, lambda i, nd=a.ndim: (0,) * nd)
    return pl.pallas_call(
        functools.partial(_mix_sample_body, t_new=t_new, past_len=past_len),
        grid=(1,),
        in_specs=[full(a) for a in args],
        out_specs=[pl.BlockSpec((n, D_MODEL), lambda i: (0, 0)), pl.BlockSpec((bs, state_w), lambda i: (0, 0))],
        out_shape=[jax.ShapeDtypeStruct((n, D_MODEL), F32), jax.ShapeDtypeStruct((bs, state_w), F32)],
        scratch_shapes=[
            pltpu.VMEM((state_w, LANES), F32),
            pltpu.VMEM((t_new, 128, LANES), F32), pltpu.VMEM((t_new, 128, LANES), F32),
            pltpu.VMEM((t_new, 128, LANES), F32), pltpu.VMEM((t_new, GLA_WIDTH, LANES), F32),
            pltpu.VMEM((t_new, GLA_WIDTH, LANES), F32), pltpu.VMEM((n, GLA_WIDTH), F32),
        ],
        compiler_params=pltpu.CompilerParams(dimension_semantics=("arbitrary",),
                                             vmem_limit_bytes=VMEM_LIMIT_BYTES),
        name="mix_sample",
    )(*args)


def _rope_table(pos):
    half = MLA_ROPE // 2
    inv = 1.0 / (ROPE_BASE ** (jnp.arange(half, dtype=F32) / half))
    ang = pos.astype(F32)[:, None] * inv[None, :]
    cos, sin = jnp.cos(ang), jnp.sin(ang)
    return jnp.concatenate([cos, cos, -sin, sin], axis=1)


def _layer_weights(l, g_pre, w_in, g_cq, w_uq, g_ckv, w_uk, w_uv, w_pool, pool_scale, w_a2, b_a, g_gla,
                   w_out, g_post):
    half = MLA_ROPE // 2
    wi = w_in[l]
    w_in_ext = jnp.concatenate([
        wi[:, 0:576],
        wi[:, 544:576], wi[:, 512:544],
        wi[:, 576:2112],
        wi[:, 2128:2384],
        wi[:, 2112:2128],
        jnp.zeros((D_MODEL, LANES - GLA_GATE_RANK), F32),
    ], axis=1).astype(BF16)
    wq = w_uq[l]
    rope_cols = wq[:, :, MLA_NOPE:]
    w_uq_ext = jnp.concatenate([wq, rope_cols[:, :, half:], rope_cols[:, :, :half]], axis=2)
    w_uq_ext = w_uq_ext.reshape(Q_RANK, MLA_HEADS * 256).astype(BF16)
    w_pool_bd = jnp.zeros((POOL_WIDTH, POOL_WIDTH), F32)
    for g in range(len(POOL_WINDOWS)):
        w_pool_bd = w_pool_bd.at[g * POOL_GC:(g + 1) * POOL_GC, g * POOL_GC:(g + 1) * POOL_GC].set(w_pool[l, g])
    w_a2_pad = jnp.zeros((LANES, GLA_HEADS * GLA_DK), F32).at[:GLA_GATE_RANK].set(w_a2[l])
    return {
        "g_pre": g_pre[l][None, :], "w_in": w_in_ext, "g_cq": g_cq[l][None, :], "w_uq": w_uq_ext,
        "g_ckv": g_ckv[l][None, :],
        "w_uk": jnp.transpose(w_uk[l], (1, 2, 0)).astype(BF16),
        "w_uv": jnp.transpose(w_uv[l], (1, 0, 2)).astype(BF16),
        "w_pool": w_pool_bd.astype(BF16), "pool_scale": pool_scale[l][None, :],
        "w_a2": w_a2_pad.astype(BF16), "b_a": b_a[l][None, :], "g_gla": g_gla[l][None, :],
        "w_out": w_out[l].astype(BF16), "g_post": g_post[l][None, :],
    }


def _tile_sizes(seq):
    pick = lambda want: max(t for t in (64, 128, 256, 512) if t <= want and seq % t == 0)
    return pick(512), pick(256), pick(256)


@jax.jit
def _forward(x_prompt, x_sample, cache_ckv, cache_krope, state_pool, state_gla, page_table,
             g_pre, w_in, g_cq, w_uq, g_ckv, w_uk, w_uv, w_pool, pool_scale, w_a2, b_a, g_gla, w_out, g_post):
    bp, tp, _ = x_prompt.shape
    bs, ts, _ = x_sample.shape
    depth = w_in.shape[0]
    n_pages = page_table.shape[1]
    past_len = n_pages * PAGE_SIZE
    tm_in, tq, tm_mix = _tile_sizes(tp)
    pages_per_step = 8 if n_pages % 8 == 0 else 1

    tab_p = _rope_table(jnp.arange(tp))
    tab_s = jnp.repeat(_rope_table(past_len + jnp.arange(ts)), bs, axis=0)
    pt_flat = page_table.reshape(-1).astype(jnp.int32)

    hp = x_prompt.reshape(bp * tp, D_MODEL)
    hs = jnp.transpose(x_sample, (1, 0, 2)).reshape(ts * bs, D_MODEL)
    outs = {k: [] for k in ("ckv_p", "kr_p", "pool_p", "gla_p", "ckv_s", "kr_s", "pool_s", "gla_s")}
    for l in range(depth):
        w = _layer_weights(l, g_pre, w_in, g_cq, w_uq, g_ckv, w_uk, w_uv, w_pool, pool_scale, w_a2, b_a,
                           g_gla, w_out, g_post)
        ql, qr, ckv, kr, ckvb, krb, rest = _inproj(hp, tab_p, w, tm=tm_in)
        omla = _attn_prompt(ql, qr, ckvb, krb, w["w_uv"], batch=bp, seq=tp, tq=tq)
        hp, st = _mix_prompt(rest, omla, hp, w, batch=bp, seq=tp, tm=tm_mix)
        outs["ckv_p"].append(ckv.reshape(bp, tp, KV_RANK))
        outs["kr_p"].append(kr.reshape(bp, tp, MLA_ROPE))
        outs["pool_p"].append(rest.reshape(bp, tp, D_REST)[:, tp - POOL_BUF:, R_POOL:R_POOL + POOL_WIDTH])
        st = st.reshape(bp, GLA_HEADS, GLA_DV, GLA_HEADS, GLA_DK)
        st = jnp.stack([st[:, h, :, h, :] for h in range(GLA_HEADS)], axis=1)
        outs["gla_p"].append(jnp.swapaxes(st, 2, 3))
        ql, qr, ckv, kr, ckvb, krb, rest = _inproj(hs, tab_s, w, tm=ts * bs)
        to_seq = lambda a, width: jnp.transpose(a.reshape(MLA_HEADS, ts, bs, width), (2, 0, 1, 3)).reshape(
            bs, MLA_HEADS * ts, width)
        ckv_b = jnp.transpose(ckv.reshape(ts, bs, KV_RANK), (1, 0, 2))
        kr_b = jnp.transpose(kr.reshape(ts, bs, MLA_ROPE), (1, 0, 2))
        olat = _attn_sample(pt_flat, to_seq(ql, KV_RANK), to_seq(qr, LANES), ckv_b, kr_b, cache_ckv, cache_krope,
                            layer=l, n_pages=n_pages, pages=pages_per_step)
        olat = jnp.transpose(olat.reshape(bs, MLA_HEADS, ts, KV_RANK), (1, 2, 0, 3)).reshape(
            MLA_HEADS, ts * bs, KV_RANK)
        prefix_t = jnp.transpose(state_pool[l], (1, 0, 2))
        s0 = state_gla[l].reshape(bs, GLA_HEADS * GLA_DK * GLA_DV)
        hs, s_new = _mix_sample(rest, olat, prefix_t, s0, hs, w, t_new=ts, past_len=past_len)
        xp_b = jnp.transpose(rest[:, R_POOL:R_POOL + POOL_WIDTH].reshape(ts, bs, POOL_WIDTH), (1, 0, 2))
        outs["ckv_s"].append(ckv_b)
        outs["kr_s"].append(kr_b)
        outs["pool_s"].append(jnp.concatenate([state_pool[l], xp_b], axis=1)[:, ts:])
        outs["gla_s"].append(s_new.reshape(bs, GLA_HEADS, GLA_DK, GLA_DV))
    y_prompt = hp.reshape(bp, tp, D_MODEL)
    y_sample = jnp.transpose(hs.reshape(ts, bs, D_MODEL), (1, 0, 2))
    return (y_prompt, y_sample,
            jnp.stack(outs["ckv_p"]), jnp.stack(outs["kr_p"]), jnp.stack(outs["pool_p"]), jnp.stack(outs["gla_p"]),
            jnp.stack(outs["ckv_s"]), jnp.stack(outs["kr_s"]), jnp.stack(outs["pool_s"]), jnp.stack(outs["gla_s"]))


def kernel(x_prompt, x_sample, cache_ckv, cache_krope, state_pool, state_gla, page_table, g_pre, w_in, g_cq,
           w_uq, g_ckv, w_uk, w_uv, w_pool, pool_scale, w_a2, b_a, g_gla, w_out, g_post):
    return _forward(x_prompt, x_sample, cache_ckv, cache_krope, state_pool, state_gla, page_table, g_pre, w_in,
                    g_cq, w_uq, g_ckv, w_uk, w_uv, w_pool, pool_scale, w_a2, b_a, g_gla, w_out, g_post)
```

```python
import functools

import jax
import jax.numpy as jnp
from jax import lax
from jax.experimental import pallas as pl
from jax.experimental.pallas import tpu as pltpu

F32 = jnp.float32
BF16 = jnp.bfloat16

D_MODEL = 1024
MLA_HEADS = 4
MLA_NOPE = 128
MLA_ROPE = 64
MLA_V = 128
Q_RANK = 256
KV_RANK = 256
ROPE_BASE = 10000.0
POOL_WINDOWS = (2, 4, 8, 16)
POOL_WIDTH = 256
POOL_GC = 64
POOL_BUF = 15
GLA_HEADS = 4
GLA_DK = 32
GLA_DV = 64
GLA_WIDTH = 256
GLA_GATE_RANK = 16
GLA_TAU = 16.0
GLA_CHUNK = 64
NORM_EPS = 1e-6
PAGE_SIZE = 128
ATTN_SCALE = (MLA_NOPE + MLA_ROPE) ** -0.5
GLA_SCALE = GLA_DK ** -0.5

LANES = 128
VMEM_LIMIT_BYTES = 56 * 1024 * 1024

C_CQ = 0
C_CKV = 256
C_KR = 512
C_REST = 640
C_A = 2432
D_EXT = 2560
R_GMLA = 0
R_POOL = 512
R_GPOOL = 768
R_Q = 1024
R_K = 1152
R_V = 1280
R_GGLA = 1536
R_LOGA = 1792
D_REST = 1920
D_QK = KV_RANK + LANES

NEG_BIG = -1e30


def _dot(a, b):
    return jnp.dot(a, b, preferred_element_type=F32)


def _dot_nt(a, b):
    return lax.dot_general(a, b, (((1,), (1,)), ((), ())), preferred_element_type=F32)


def _dot_tn(a, b):
    return lax.dot_general(a, b, (((0,), (0,)), ((), ())), preferred_element_type=F32)


def _rms(x, g):
    return x * lax.rsqrt(jnp.mean(x * x, axis=-1, keepdims=True) + NORM_EPS) * g


def _silu(x):
    return x / (1.0 + jnp.exp(-x))


def _split3(x):
    hi = x.astype(BF16)
    r = x - hi.astype(F32)
    mid = r.astype(BF16)
    lo = (r - mid.astype(F32)).astype(BF16)
    return hi, mid, lo


def _dot3(mat, parts):
    return _dot(mat, parts[0]) + _dot(mat, parts[1]) + _dot(mat, parts[2])


def _iota(shape, dim):
    return lax.broadcasted_iota(jnp.int32, shape, dim)


def _inproj_body(x_ref, tab_ref, gpre_ref, win_ref, gcq_ref, wuq_ref, gckv_ref, wuk_ref,
                 wa2_ref, ba_ref,
                 qc_ref, ckv_ref, kr_ref, kc_ref, rest_ref):
    u = _rms(x_ref[...], gpre_ref[...])
    p = _dot(u.astype(BF16), win_ref[...])
    tab = tab_ref[...]
    low_half = (_iota((1, LANES), 1) < MLA_ROPE).astype(F32)

    def rope(slab):
        t = slab * tab
        return t + pltpu.roll(t, MLA_ROPE, axis=1)

    cq = _rms(p[:, C_CQ:C_CQ + Q_RANK], gcq_ref[...])
    q = _dot(cq.astype(BF16), wuq_ref[...])
    for h in range(MLA_HEADS):
        base = h * 256
        q_nope = q[:, base:base + MLA_NOPE]
        q_lat = _dot(q_nope.astype(BF16), wuk_ref[h]) * ATTN_SCALE
        q_rope = rope(q[:, base + MLA_NOPE:base + 256]) * (low_half * ATTN_SCALE)
        qc_ref[h] = jnp.concatenate([q_lat, q_rope], axis=1).astype(BF16)

    ckv = _rms(p[:, C_CKV:C_CKV + KV_RANK], gckv_ref[...])
    ckv_ref[...] = ckv
    kr = rope(p[:, C_KR:C_KR + LANES])
    kr_ref[...] = kr[:, :MLA_ROPE]
    kc_ref[...] = jnp.concatenate([ckv, kr * low_half], axis=1).astype(BF16)

    rest_ref[:, 0:R_LOGA] = p[:, C_REST:C_A]
    z = _dot(p[:, C_A:D_EXT].astype(BF16), wa2_ref[...]) + ba_ref[...]
    log_sig = jnp.minimum(z, 0.0) - jnp.log(1.0 + jnp.exp(-jnp.abs(z)))
    rest_ref[:, R_LOGA:D_REST] = log_sig * (1.0 / GLA_TAU)


def _inproj(x, tab, w, *, tm):
    n = x.shape[0]
    nt = tab.shape[0] // tm
    full = lambda a: pl.BlockSpec(a.shape, lambda i, nd=a.ndim: (0,) * nd)
    row = lambda width: pl.BlockSpec((tm, width), lambda i: (i, 0))
    head = lambda width: pl.BlockSpec((MLA_HEADS, tm, width), lambda i: (0, i, 0))
    weights = (w["g_pre"], w["w_in"], w["g_cq"], w["w_uq"], w["g_ckv"], w["w_uk"], w["w_a2"], w["b_a"])
    return pl.pallas_call(
        _inproj_body,
        grid=(n // tm,),
        in_specs=[row(D_MODEL), pl.BlockSpec((tm, LANES), lambda i: (i % nt, 0))]
        + [full(a) for a in weights],
        out_specs=[head(D_QK), row(KV_RANK), row(MLA_ROPE), row(D_QK), row(D_REST)],
        out_shape=[
            jax.ShapeDtypeStruct((MLA_HEADS, n, D_QK), BF16),
            jax.ShapeDtypeStruct((n, KV_RANK), F32),
            jax.ShapeDtypeStruct((n, MLA_ROPE), F32),
            jax.ShapeDtypeStruct((n, D_QK), BF16),
            jax.ShapeDtypeStruct((n, D_REST), F32),
        ],
        compiler_params=pltpu.CompilerParams(dimension_semantics=("parallel",),
                                             vmem_limit_bytes=VMEM_LIMIT_BYTES),
        name="inproj",
    )(x, tab, *weights)


def _attn_prompt_body(q_ref, k_ref, wuv_ref, o_ref, s_ref, m_ref, l_ref, acc_ref, *, tq):
    i = pl.program_id(1)
    rows = MLA_HEADS * tq
    q = q_ref[...].reshape(rows, D_QK)
    m_ref[...] = jnp.full((rows, LANES), NEG_BIG, F32)
    l_ref[...] = jnp.zeros((rows, LANES), F32)
    acc_ref[...] = jnp.zeros((rows, KV_RANK), F32)

    def scores(j):
        start = pl.multiple_of(j * tq, tq)
        return _dot_nt(q, k_ref[pl.ds(start, tq), :])

    def consume(j, s, diagonal):
        start = pl.multiple_of(j * tq, tq)
        v = k_ref[pl.ds(start, tq), 0:KV_RANK]
        if diagonal:
            qpos = _iota((rows, tq), 0) & (tq - 1)
            kpos = _iota((rows, tq), 1)
            s = jnp.where(kpos <= qpos, s, NEG_BIG)
        m_old = m_ref[...]
        m_new = jnp.maximum(m_old, jnp.max(s, axis=1, keepdims=True))
        alpha = jnp.exp(m_old - m_new)
        p = [jnp.exp(s[:, c * LANES:(c + 1) * LANES] - m_new) for c in range(tq // LANES)]
        l_new = alpha * l_ref[...]
        for pc in p:
            l_new = l_new + pc
        l_ref[...] = l_new
        pv = _dot(jnp.concatenate(p, axis=1).astype(BF16), v)
        acc_ref[...] = jnp.concatenate([alpha] * (KV_RANK // LANES), axis=1) * acc_ref[...] + pv
        m_ref[...] = m_new

    s_ref[0] = scores(0)

    def body(j, carry):
        slot = j & 1
        s_ref[1 - slot] = scores(j + 1)
        consume(j, s_ref[slot], False)
        return carry

    lax.fori_loop(0, i, body, 0)
    consume(i, s_ref[i & 1], True)

    l = jnp.sum(l_ref[...], axis=1, keepdims=True)
    o = acc_ref[...] * (1.0 / l)
    for h in range(MLA_HEADS):
        o_h = o[h * tq:(h + 1) * tq].astype(BF16)
        o_ref[:, h * MLA_V:(h + 1) * MLA_V] = _dot(o_h, wuv_ref[h])


def _attn_prompt(qc, kc, wuv, *, batch, seq, tq):
    n = batch * seq
    nq = seq // tq
    rows = MLA_HEADS * tq
    return pl.pallas_call(
        functools.partial(_attn_prompt_body, tq=tq),
        grid=(batch, nq),
        in_specs=[
            pl.BlockSpec((MLA_HEADS, tq, D_QK), lambda b, i: (0, b * nq + i, 0)),
            pl.BlockSpec((seq, D_QK), lambda b, i: (b, 0)),
            pl.BlockSpec(wuv.shape, lambda b, i: (0, 0, 0)),
        ],
        out_specs=pl.BlockSpec((tq, MLA_HEADS * MLA_V), lambda b, i: (b * nq + i, 0)),
        out_shape=jax.ShapeDtypeStruct((n, MLA_HEADS * MLA_V), F32),
        scratch_shapes=[pltpu.VMEM((2, rows, tq), F32), pltpu.VMEM((rows, LANES), F32),
                        pltpu.VMEM((rows, LANES), F32), pltpu.VMEM((rows, KV_RANK), F32)],
        compiler_params=pltpu.CompilerParams(dimension_semantics=("parallel", "arbitrary"),
                                             vmem_limit_bytes=VMEM_LIMIT_BYTES),
        name="attn_prompt",
    )(qc, kc, wuv)


ROPE_STACK = 2 * LANES // MLA_ROPE


ROPE_SUB = 2 * ROPE_STACK


def _attn_sample_body(pt_ref, q_ref, cn_ref, krn_ref, ckv_hbm, kr_hbm, o_ref, kv_buf, kr_buf, sem,
                      *, layer, n_pages, pages, t_new):
    b = pl.program_id(0)
    n_chunks = n_pages // pages
    rows = MLA_HEADS * t_new

    def page_copies(page, i, slot):
        k, p_i = i // ROPE_SUB, i % ROPE_SUB
        g, u = p_i % ROPE_STACK, p_i // ROPE_STACK
        pos = k * ROPE_SUB + g * 2 + u
        return (
            pltpu.make_async_copy(ckv_hbm.at[layer, page],
                                  kv_buf.at[slot, pl.ds(pos * PAGE_SIZE, PAGE_SIZE), :], sem.at[0, slot]),
            pltpu.make_async_copy(kr_hbm.at[layer, page],
                                  kr_buf.at[slot, pl.ds(g * MLA_ROPE, MLA_ROPE),
                                            pl.ds((2 * k + u) * PAGE_SIZE, PAGE_SIZE)], sem.at[1, slot]),
        )

    def start_chunk(seq, c, slot):
        for i in range(pages):
            for cp in page_copies(pt_ref[seq * n_pages + c * pages + i], i, slot):
                cp.start()

    def wait_chunk(slot):
        for i in range(pages):
            for cp in page_copies(0, i, slot):
                cp.wait()

    @pl.when(b == 0)
    def _():
        start_chunk(0, 0, 0)

    ql = q_ref[0, :, 0:KV_RANK]
    qlf = ql.astype(F32)
    qr = q_ref[0, :, KV_RANK:D_QK].astype(F32)
    qr2 = qr + pltpu.roll(qr, MLA_ROPE, axis=1)
    q_wide = jnp.concatenate([qr2, qr2], axis=1)
    bd_shape = (ROPE_STACK * rows, ROPE_STACK * MLA_ROPE)
    on_block = (_iota(bd_shape, 0) >> 4) == (_iota(bd_shape, 1) >> 6)
    q_bd = jnp.where(on_block, jnp.concatenate([q_wide] * ROPE_STACK, axis=0), 0.0).astype(BF16)

    qrf = qr[:, 0:MLA_ROPE]
    cn = cn_ref[0]
    krn = krn_ref[0]
    t_row = _iota((rows, 1), 0) & (t_new - 1)
    scores = []
    for t2 in range(t_new):
        sc = (jnp.sum(qlf * cn[t2:t2 + 1, :], axis=1, keepdims=True)
              + jnp.sum(qrf * krn[t2:t2 + 1, :], axis=1, keepdims=True))
        scores.append(jnp.where(t2 <= t_row, sc, NEG_BIG))
    m = scores[0]
    for sc in scores[1:]:
        m = jnp.maximum(m, sc)
    l = jnp.zeros((rows, 1), F32)
    acc = jnp.zeros((rows, KV_RANK), F32)
    for t2 in range(t_new):
        p = jnp.exp(scores[t2] - m)
        l = l + p
        acc = acc + p * cn[t2:t2 + 1, :]

    for c in range(n_chunks):
        slot = c % 2
        if c + 1 < n_chunks:
            start_chunk(b, c + 1, 1 - slot)
        else:
            @pl.when(b + 1 < pl.num_programs(0))
            def _():
                start_chunk(b + 1, 0, 1 - slot)
        wait_chunk(slot)
        kbs, s_parts = [], []
        for k in range(pages // ROPE_SUB):
            kb = kv_buf[slot, k * ROPE_SUB * PAGE_SIZE:(k + 1) * ROPE_SUB * PAGE_SIZE, :].astype(BF16)
            rb = kr_buf[slot, :, 2 * k * PAGE_SIZE:2 * (k + 1) * PAGE_SIZE].astype(BF16)
            s_rope = _dot(q_bd, rb)
            s_rope = jnp.concatenate([s_rope[g * rows:(g + 1) * rows] for g in range(ROPE_STACK)], axis=1)
            s_parts.append(_dot_nt(ql, kb) + s_rope)
            kbs.append(kb)
        s = jnp.concatenate(s_parts, axis=1)
        m_new = jnp.maximum(m, jnp.max(s, axis=1, keepdims=True))
        alpha = jnp.exp(m - m_new)
        p = jnp.exp(s - m_new)
        l = alpha * l + jnp.sum(p, axis=1, keepdims=True)
        acc = alpha * acc
        for k, kb in enumerate(kbs):
            acc = acc + _dot(p[:, k * ROPE_SUB * PAGE_SIZE:(k + 1) * ROPE_SUB * PAGE_SIZE].astype(BF16), kb)
        m = m_new

    o_ref[0] = acc * (1.0 / l)


def _attn_sample(page_table_flat, q, ckv_new, kr_new, cache_ckv, cache_krope_t, *, layer, n_pages, pages):
    bs, rows, _ = q.shape
    t_new = ckv_new.shape[1]
    assert (n_pages // pages) % 2 == 0 and pages % ROPE_SUB == 0
    grid_spec = pltpu.PrefetchScalarGridSpec(
        num_scalar_prefetch=1,
        grid=(bs,),
        in_specs=[
            pl.BlockSpec((1, rows, D_QK), lambda b, pt: (b, 0, 0)),
            pl.BlockSpec((1, t_new, KV_RANK), lambda b, pt: (b, 0, 0)),
            pl.BlockSpec((1, t_new, MLA_ROPE), lambda b, pt: (b, 0, 0)),
            pl.BlockSpec(memory_space=pl.ANY),
            pl.BlockSpec(memory_space=pl.ANY),
        ],
        out_specs=pl.BlockSpec((1, rows, KV_RANK), lambda b, pt: (b, 0, 0)),
        scratch_shapes=[
            pltpu.VMEM((2, pages * PAGE_SIZE, KV_RANK), F32),
            pltpu.VMEM((2, ROPE_STACK * MLA_ROPE, (pages // ROPE_STACK) * PAGE_SIZE), F32),
            pltpu.SemaphoreType.DMA((2, 2)),
        ],
    )
    return pl.pallas_call(
        functools.partial(_attn_sample_body, layer=layer, n_pages=n_pages, pages=pages, t_new=t_new),
        grid_spec=grid_spec,
        out_shape=jax.ShapeDtypeStruct((bs, rows, KV_RANK), F32),
        compiler_params=pltpu.CompilerParams(dimension_semantics=("arbitrary",),
                                             vmem_limit_bytes=VMEM_LIMIT_BYTES),
        name="attn_sample",
    )(page_table_flat, q, ckv_new, kr_new, cache_ckv, cache_krope_t)


def _head_rms(o, g):
    o2 = o * o
    hi = o2.astype(BF16)
    lo = (o2 - hi.astype(F32)).astype(BF16)
    same_head = (_iota((GLA_WIDTH, GLA_WIDTH), 0) >> 6) == (_iota((GLA_WIDTH, GLA_WIDTH), 1) >> 6)
    ones_bd = same_head.astype(BF16)
    ms = (_dot(hi, ones_bd) + _dot(lo, ones_bd)) * (1.0 / GLA_DV)
    return o * lax.rsqrt(ms + NORM_EPS) * g


def _pool_select(s2, s4, s8, s16, count, xp):
    group = _iota((1, POOL_WIDTH), 1) >> 6
    win = jnp.where(group == 0, s2, jnp.where(group == 1, s4, jnp.where(group == 2, s8, s16)))
    return win / count - xp


def _gate_and_project(o_mla, rest, o_pool, o_gla, h, wout_ref, gpost):
    mixed = jnp.concatenate([
        o_mla * _silu(rest[:, R_GMLA:R_GMLA + 512]),
        o_pool * _silu(rest[:, R_GPOOL:R_GPOOL + POOL_WIDTH]),
        o_gla * _silu(rest[:, R_GGLA:R_GGLA + GLA_WIDTH]),
    ], axis=1).astype(BF16)
    y = _dot(mixed, wout_ref[...])
    return h + _rms(y, gpost)


def _mix_prompt_body(rest_ref, omla_ref, h_ref, wpool_ref, pscale_ref, ggla_ref, wout_ref, gpost_ref,
                     hout_ref, sout_ref, hist_ref, st_ref, o_ref, *, tm):
    t = pl.program_id(1)
    n_chunks = tm // GLA_CHUNK

    @pl.when(t == 0)
    def _():
        hist_ref[...] = jnp.zeros(hist_ref.shape, F32)
        st_ref[...] = jnp.zeros(st_ref.shape, F32)

    rest = rest_ref[...]

    xp = rest[:, R_POOL:R_POOL + POOL_WIDTH]
    row = _iota((tm, POOL_WIDTH), 0)

    def shifted(cur, slot, k):
        prev = hist_ref[slot]
        hist_ref[slot] = cur
        return jnp.where(row >= k, pltpu.roll(cur, k, axis=0), pltpu.roll(prev, k, axis=0))

    s2 = xp + shifted(xp, 0, 1)
    s4 = s2 + shifted(s2, 1, 2)
    s8 = s4 + shifted(s4, 2, 4)
    s16 = s8 + shifted(s8, 3, 8)
    window = 2 << (_iota((1, POOL_WIDTH), 1) >> 6)
    count = jnp.minimum(t * tm + row + 1, window).astype(F32)
    pooled = _pool_select(s2, s4, s8, s16, count, xp)
    o_pool = _dot(pooled.astype(BF16), wpool_ref[...]) * pscale_ref[...]

    q = rest[:, R_Q:R_Q + 128]
    k = rest[:, R_K:R_K + 128]
    v = rest[:, R_V:R_V + GLA_WIDTH]
    la = rest[:, R_LOGA:R_LOGA + 128]
    r_i = _iota((tm, tm), 0)
    c_i = _iota((tm, tm), 1)
    same_chunk = (r_i >> 6) == (c_i >> 6)
    tri_bd = (same_chunk & (c_i <= r_i)).astype(BF16)
    ones_bd = same_chunk.astype(BF16)
    la_parts = _split3(la)
    b = _dot3(tri_bd, la_parts)
    b_last = _dot3(ones_bd, la_parts)
    qt = q * GLA_SCALE * jnp.exp(b)
    kt = k * jnp.exp(-b)
    kd = k * jnp.exp(b_last - b)
    dec = jnp.exp(b_last)
    head_mask = ((_iota((4 * GLA_CHUNK, 128), 0) >> 6) == (_iota((4 * GLA_CHUNK, 128), 1) >> 5)).astype(F32)
    causal = (_iota((4 * GLA_CHUNK, GLA_CHUNK), 0) & (GLA_CHUNK - 1)) >= _iota((4 * GLA_CHUNK, GLA_CHUNK), 1)
    lane_head = _iota((GLA_CHUNK, GLA_WIDTH), 1) >> 6
    for n in range(n_chunks):
        lo, hi = n * GLA_CHUNK, (n + 1) * GLA_CHUNK
        q_n = qt[lo:hi]
        v_n = v[lo:hi].astype(BF16)
        q_bd = (jnp.concatenate([q_n] * GLA_HEADS, axis=0) * head_mask).astype(BF16)
        a = _dot_nt(q_bd, kt[lo:hi].astype(BF16))
        a = jnp.where(causal, a, 0.0)
        pv = _dot(a.astype(BF16), v_n)
        o_n = jnp.zeros((GLA_CHUNK, GLA_WIDTH), F32)
        for hh in range(GLA_HEADS):
            o_n = o_n + jnp.where(lane_head == hh, pv[hh * GLA_CHUNK:(hh + 1) * GLA_CHUNK], 0.0)
        st = st_ref[...]
        o_n = o_n + _dot_nt(q_n.astype(BF16), st.astype(BF16))
        upd = _dot_tn(v_n, kd[lo:hi].astype(BF16))
        st_ref[...] = dec[lo:lo + 1, :] * st + upd * head_mask
        o_ref[lo:hi, :] = o_n
    o_gla = _head_rms(o_ref[...], ggla_ref[...])

    hout_ref[...] = _gate_and_project(omla_ref[...], rest, o_pool, o_gla, h_ref[...], wout_ref,
                                      gpost_ref[...])

    @pl.when(t == pl.num_programs(1) - 1)
    def _():
        sout_ref[...] = st_ref[...]


def _mix_prompt(rest, omla, h, w, *, batch, seq, tm):
    n = batch * seq
    nt = seq // tm
    full = lambda a: pl.BlockSpec(a.shape, lambda b, t, nd=a.ndim: (0,) * nd)
    row = lambda width: pl.BlockSpec((tm, width), lambda b, t: (b * nt + t, 0))
    weights = (w["w_pool"], w["pool_scale"], w["g_gla"], w["w_out"], w["g_post"])
    return pl.pallas_call(
        functools.partial(_mix_prompt_body, tm=tm),
        grid=(batch, nt),
        in_specs=[row(D_REST), row(512), row(D_MODEL)] + [full(a) for a in weights],
        out_specs=[row(D_MODEL), pl.BlockSpec((None, 4 * GLA_DV, 128), lambda b, t: (b, 0, 0))],
        out_shape=[jax.ShapeDtypeStruct((n, D_MODEL), F32),
                   jax.ShapeDtypeStruct((batch, 4 * GLA_DV, 128), F32)],
        scratch_shapes=[pltpu.VMEM((4, tm, POOL_WIDTH), F32), pltpu.VMEM((4 * GLA_DV, 128), F32),
                        pltpu.VMEM((tm, GLA_WIDTH), F32)],
        compiler_params=pltpu.CompilerParams(dimension_semantics=("parallel", "arbitrary"),
                                             vmem_limit_bytes=VMEM_LIMIT_BYTES),
        name="mix_prompt",
    )(rest, omla, h, *weights)


def _mix_sample_body(rest_ref, olat_ref, prefix_ref, s0_ref, h_ref, wuv_ref, wpool_ref, pscale_ref,
                     ggla_ref, wout_ref, gpost_ref,
                     hout_ref, snew_ref, st_ref, qT_ref, kT_ref, aT_ref, vT_ref, oT_ref, o_ref,
                     *, t_new, past_len):
    bs = LANES
    n_state_blocks = (GLA_HEADS * GLA_DK * GLA_DV) // LANES
    rest = rest_ref[...]

    o_mla = jnp.concatenate([_dot(olat_ref[h].astype(BF16), wuv_ref[h]) for h in range(MLA_HEADS)], axis=1)

    z = [prefix_ref[j] for j in range(POOL_BUF)]
    z += [rest[t * bs:(t + 1) * bs, R_POOL:R_POOL + POOL_WIDTH] for t in range(t_new)]
    pooled = []
    for t in range(t_new):
        e = POOL_BUF + t
        s2 = z[e] + z[e - 1]
        s4 = s2 + z[e - 2] + z[e - 3]
        s8 = s4 + z[e - 4] + z[e - 5] + z[e - 6] + z[e - 7]
        s16 = s8
        for i in range(8, 16):
            s16 = s16 + z[e - i]
        window = 2 << (_iota((1, POOL_WIDTH), 1) >> 6)
        count = jnp.minimum(past_len + t + 1, window).astype(F32)
        pooled.append(_pool_select(s2, s4, s8, s16, count, z[e]))
    pooled = jnp.concatenate(pooled, axis=0)
    o_pool = _dot(pooled.astype(BF16), wpool_ref[...]) * pscale_ref[...]

    for j in range(n_state_blocks):
        st_ref[j * LANES:(j + 1) * LANES, :] = s0_ref[:, j * LANES:(j + 1) * LANES].T
    for t in range(t_new):
        rows = slice(t * bs, (t + 1) * bs)
        qT_ref[t] = (rest[rows, R_Q:R_Q + 128] * GLA_SCALE).T
        kT_ref[t] = rest[rows, R_K:R_K + 128].T
        aT_ref[t] = jnp.exp(rest[rows, R_LOGA:R_LOGA + 128]).T
        vT_ref[t, 0:LANES] = rest[rows, R_V:R_V + LANES].T
        vT_ref[t, LANES:2 * LANES] = rest[rows, R_V + LANES:R_V + 2 * LANES].T
    for hh in range(GLA_HEADS):
        v_blocks = [vT_ref[t, hh * GLA_DV:(hh + 1) * GLA_DV, :] for t in range(t_new)]

        def body(d, carry, hh=hh, v_blocks=v_blocks):
            r = hh * GLA_DK + d
            base = pl.multiple_of(r * GLA_DV, GLA_DV)
            s = st_ref[pl.ds(base, GLA_DV), :]
            outs = []
            for t in range(t_new):
                s = aT_ref[t, pl.ds(r, 1), :] * s + kT_ref[t, pl.ds(r, 1), :] * v_blocks[t]
                outs.append(carry[t] + qT_ref[t, pl.ds(r, 1), :] * s)
            st_ref[pl.ds(base, GLA_DV), :] = s
            return tuple(outs)

        acc = lax.fori_loop(0, GLA_DK, body, tuple(jnp.zeros((GLA_DV, LANES), F32) for _ in range(t_new)))
        for t in range(t_new):
            oT_ref[t, hh * GLA_DV:(hh + 1) * GLA_DV, :] = acc[t]
    for j in range(n_state_blocks):
        snew_ref[:, j * LANES:(j + 1) * LANES] = st_ref[j * LANES:(j + 1) * LANES, :].T
    for t in range(t_new):
        o_ref[t * bs:(t + 1) * bs, 0:LANES] = oT_ref[t, 0:LANES, :].T
        o_ref[t * bs:(t + 1) * bs, LANES:2 * LANES] = oT_ref[t, LANES:2 * LANES, :].T
    o_gla = _head_rms(o_ref[...], ggla_ref[...])

    hout_ref[...] = _gate_and_project(o_mla, rest, o_pool, o_gla, h_ref[...], wout_ref, gpost_ref[...])


def _mix_sample(rest, olat, prefix_t, s0, h, w, *, t_new, past_len):
    n = rest.shape[0]
    bs = n // t_new
    assert bs == LANES, "sample mixing kernel keeps the sequences on the lane axis"
    state_w = GLA_HEADS * GLA_DK * GLA_DV
    args = (rest, olat, prefix_t, s0, h, w["w_uv"], w["w_pool"], w["pool_scale"], w["g_gla"], w["w_out"],
            w["g_post"])
    full = lambda a: pl.BlockSpec(a.shape, lambda i, nd=a.ndim: (0,) * nd)
    return pl.pallas_call(
        functools.partial(_mix_sample_body, t_new=t_new, past_len=past_len),
        grid=(1,),
        in_specs=[full(a) for a in args],
        out_specs=[pl.BlockSpec((n, D_MODEL), lambda i: (0, 0)), pl.BlockSpec((bs, state_w), lambda i: (0, 0))],
        out_shape=[jax.ShapeDtypeStruct((n, D_MODEL), F32), jax.ShapeDtypeStruct((bs, state_w), F32)],
        scratch_shapes=[
            pltpu.VMEM((state_w, LANES), F32),
            pltpu.VMEM((t_new, 128, LANES), F32), pltpu.VMEM((t_new, 128, LANES), F32),
            pltpu.VMEM((t_new, 128, LANES), F32), pltpu.VMEM((t_new, GLA_WIDTH, LANES), F32),
            pltpu.VMEM((t_new, GLA_WIDTH, LANES), F32), pltpu.VMEM((n, GLA_WIDTH), F32),
        ],
        compiler_params=pltpu.CompilerParams(dimension_semantics=("arbitrary",),
                                             vmem_limit_bytes=VMEM_LIMIT_BYTES),
        name="mix_sample",
    )(*args)


def _rope_table(pos):
    half = MLA_ROPE // 2
    inv = 1.0 / (ROPE_BASE ** (jnp.arange(half, dtype=F32) / half))
    ang = pos.astype(F32)[:, None] * inv[None, :]
    cos, sin = jnp.cos(ang), jnp.sin(ang)
    return jnp.concatenate([cos, cos, -sin, sin], axis=1)


def _layer_weights(l, g_pre, w_in, g_cq, w_uq, g_ckv, w_uk, w_uv, w_pool, pool_scale, w_a2, b_a, g_gla,
                   w_out, g_post):
    half = MLA_ROPE // 2
    wi = w_in[l]
    w_in_ext = jnp.concatenate([
        wi[:, 0:576],
        wi[:, 544:576], wi[:, 512:544],
        wi[:, 576:2112],
        wi[:, 2128:2384],
        wi[:, 2112:2128],
        jnp.zeros((D_MODEL, LANES - GLA_GATE_RANK), F32),
    ], axis=1).astype(BF16)
    wq = w_uq[l]
    rope_cols = wq[:, :, MLA_NOPE:]
    w_uq_ext = jnp.concatenate([wq, rope_cols[:, :, half:], rope_cols[:, :, :half]], axis=2)
    w_uq_ext = w_uq_ext.reshape(Q_RANK, MLA_HEADS * 256).astype(BF16)
    w_pool_bd = jnp.zeros((POOL_WIDTH, POOL_WIDTH), F32)
    for g in range(len(POOL_WINDOWS)):
        w_pool_bd = w_pool_bd.at[g * POOL_GC:(g + 1) * POOL_GC, g * POOL_GC:(g + 1) * POOL_GC].set(w_pool[l, g])
    w_a2_pad = jnp.zeros((LANES, GLA_HEADS * GLA_DK), F32).at[:GLA_GATE_RANK].set(w_a2[l])
    return {
        "g_pre": g_pre[l][None, :], "w_in": w_in_ext, "g_cq": g_cq[l][None, :], "w_uq": w_uq_ext,
        "g_ckv": g_ckv[l][None, :],
        "w_uk": jnp.transpose(w_uk[l], (1, 2, 0)).astype(BF16),
        "w_uv": jnp.transpose(w_uv[l], (1, 0, 2)).astype(BF16),
        "w_pool": w_pool_bd.astype(BF16), "pool_scale": pool_scale[l][None, :],
        "w_a2": w_a2_pad.astype(BF16), "b_a": b_a[l][None, :], "g_gla": g_gla[l][None, :],
        "w_out": w_out[l].astype(BF16), "g_post": g_post[l][None, :],
    }


def _tile_sizes(seq):
    pick = lambda want: max(t for t in (64, 128, 256, 512) if t <= want and seq % t == 0)
    return pick(512), pick(512), pick(256)


@jax.jit
def _forward(x_prompt, x_sample, cache_ckv, cache_krope, state_pool, state_gla, page_table,
             g_pre, w_in, g_cq, w_uq, g_ckv, w_uk, w_uv, w_pool, pool_scale, w_a2, b_a, g_gla, w_out, g_post):
    bp, tp, _ = x_prompt.shape
    bs, ts, _ = x_sample.shape
    depth = w_in.shape[0]
    n_pages = page_table.shape[1]
    past_len = n_pages * PAGE_SIZE
    tm_in, tq, tm_mix = _tile_sizes(tp)
    assert n_pages % (2 * ROPE_SUB) == 0 and MLA_HEADS * ts == 16
    pages_per_step = max(p for p in (8, 16, 32) if n_pages % (2 * p) == 0)
    cache_krope_t = jnp.swapaxes(cache_krope, 2, 3)

    tab_p = _rope_table(jnp.arange(tp))
    tab_s = jnp.repeat(_rope_table(past_len + jnp.arange(ts)), bs, axis=0)
    pt_flat = page_table.reshape(-1).astype(jnp.int32)

    hp = x_prompt.reshape(bp * tp, D_MODEL)
    hs = jnp.transpose(x_sample, (1, 0, 2)).reshape(ts * bs, D_MODEL)
    outs = {k: [] for k in ("ckv_p", "kr_p", "pool_p", "gla_p", "ckv_s", "kr_s", "pool_s", "gla_s")}
    for l in range(depth):
        w = _layer_weights(l, g_pre, w_in, g_cq, w_uq, g_ckv, w_uk, w_uv, w_pool, pool_scale, w_a2, b_a,
                           g_gla, w_out, g_post)
        qc, ckv, kr, kc, rest = _inproj(hp, tab_p, w, tm=tm_in)
        omla = _attn_prompt(qc, kc, w["w_uv"], batch=bp, seq=tp, tq=tq)
        hp, st = _mix_prompt(rest, omla, hp, w, batch=bp, seq=tp, tm=tm_mix)
        outs["ckv_p"].append(ckv.reshape(bp, tp, KV_RANK))
        outs["kr_p"].append(kr.reshape(bp, tp, MLA_ROPE))
        outs["pool_p"].append(rest.reshape(bp, tp, D_REST)[:, tp - POOL_BUF:, R_POOL:R_POOL + POOL_WIDTH])
        st = st.reshape(bp, GLA_HEADS, GLA_DV, GLA_HEADS, GLA_DK)
        st = jnp.stack([st[:, h, :, h, :] for h in range(GLA_HEADS)], axis=1)
        outs["gla_p"].append(jnp.swapaxes(st, 2, 3))
        qc, ckv, kr, _, rest = _inproj(hs, tab_s, w, tm=ts * bs)
        q_b = jnp.transpose(qc.reshape(MLA_HEADS, ts, bs, D_QK), (2, 0, 1, 3)).reshape(bs, MLA_HEADS * ts, D_QK)
        ckv_b = jnp.transpose(ckv.reshape(ts, bs, KV_RANK), (1, 0, 2))
        kr_b = jnp.transpose(kr.reshape(ts, bs, MLA_ROPE), (1, 0, 2))
        olat = _attn_sample(pt_flat, q_b, ckv_b, kr_b, cache_ckv, cache_krope_t, layer=l, n_pages=n_pages,
                            pages=pages_per_step)
        olat = jnp.transpose(olat.reshape(bs, MLA_HEADS, ts, KV_RANK), (1, 2, 0, 3)).reshape(
            MLA_HEADS, ts * bs, KV_RANK)
        prefix_t = jnp.transpose(state_pool[l], (1, 0, 2))
        s0 = state_gla[l].reshape(bs, GLA_HEADS * GLA_DK * GLA_DV)
        hs, s_new = _mix_sample(rest, olat, prefix_t, s0, hs, w, t_new=ts, past_len=past_len)
        xp_b = jnp.transpose(rest[:, R_POOL:R_POOL + POOL_WIDTH].reshape(ts, bs, POOL_WIDTH), (1, 0, 2))
        outs["ckv_s"].append(ckv_b)
        outs["kr_s"].append(kr_b)
        outs["pool_s"].append(jnp.concatenate([state_pool[l], xp_b], axis=1)[:, ts:])
        outs["gla_s"].append(s_new.reshape(bs, GLA_HEADS, GLA_DK, GLA_DV))
    y_prompt = hp.reshape(bp, tp, D_MODEL)
    y_sample = jnp.transpose(hs.reshape(ts, bs, D_MODEL), (1, 0, 2))
    return (y_prompt, y_sample,
            jnp.stack(outs["ckv_p"]), jnp.stack(outs["kr_p"]), jnp.stack(outs["pool_p"]), jnp.stack(outs["gla_p"]),
            jnp.stack(outs["ckv_s"]), jnp.stack(outs["kr_s"]), jnp.stack(outs["pool_s"]), jnp.stack(outs["gla_s"]))


def kernel(x_prompt, x_sample, cache_ckv, cache_krope, state_pool, state_gla, page_table, g_pre, w_in, g_cq,
           w_uq, g_ckv, w_uk, w_uv, w_pool, pool_scale, w_a2, b_a, g_gla, w_out, g_post):
    return _forward(x_prompt, x_sample, cache_ckv, cache_krope, state_pool, state_gla, page_table, g_pre, w_in,
                    g_cq, w_uq, g_ckv, w_uk, w_uv, w_pool, pool_scale, w_a2, b_a, g_gla, w_out, g_post)
```

```python
import functools

import jax
import jax.numpy as jnp
from jax import lax
from jax.experimental import pallas as pl
from jax.experimental.pallas import tpu as pltpu

F32 = jnp.float32
BF16 = jnp.bfloat16

D_MODEL = 1024
MLA_HEADS = 4
MLA_NOPE = 128
MLA_ROPE = 64
MLA_V = 128
Q_RANK = 256
KV_RANK = 256
ROPE_BASE = 10000.0
POOL_WINDOWS = (2, 4, 8, 16)
POOL_WIDTH = 256
POOL_GC = 64
POOL_BUF = 15
GLA_HEADS = 4
GLA_DK = 32
GLA_DV = 64
GLA_WIDTH = 256
GLA_GATE_RANK = 16
GLA_TAU = 16.0
GLA_CHUNK = 64
NORM_EPS = 1e-6
PAGE_SIZE = 128
QK_SCALE = (MLA_NOPE + MLA_ROPE) ** -0.5 * 1.4426950408889634
GLA_SCALE = GLA_DK ** -0.5

LANES = 128
VMEM_LIMIT_BYTES = 56 * 1024 * 1024

C_CQ = 0
C_CKV = 256
C_KR = 512
C_REST = 640
C_A = 2432
D_EXT = 2560
R_GMLA = 0
R_POOL = 512
R_GPOOL = 768
R_Q = 1024
R_K = 1152
R_V = 1280
R_GGLA = 1536
R_LOGA = 1792
D_REST = 1920
D_QK = KV_RANK + LANES

NEG_BIG = -1e30


def _dot(a, b):
    return jnp.dot(a, b, preferred_element_type=F32)


def _dot_nt(a, b):
    return lax.dot_general(a, b, (((1,), (1,)), ((), ())), preferred_element_type=F32)


def _dot_tn(a, b):
    return lax.dot_general(a, b, (((0,), (0,)), ((), ())), preferred_element_type=F32)


def _rms(x, g):
    return x * lax.rsqrt(jnp.mean(x * x, axis=-1, keepdims=True) + NORM_EPS) * g


def _silu(x):
    half = 0.5 * x
    return half + half * jnp.tanh(half)


def _split3(x):
    hi = x.astype(BF16)
    r = x - hi.astype(F32)
    mid = r.astype(BF16)
    lo = (r - mid.astype(F32)).astype(BF16)
    return hi, mid, lo


def _dot3(mat, parts):
    return _dot(mat, parts[0]) + _dot(mat, parts[1]) + _dot(mat, parts[2])


def _iota(shape, dim):
    return lax.broadcasted_iota(jnp.int32, shape, dim)


def _inproj_body(x_ref, tab_ref, gpre_ref, win_ref, gcq_ref, wuq_ref, gckv_ref, wuk_ref,
                 wa2_ref, ba_ref,
                 qc_ref, ckv_ref, kr_ref, kc_ref, rest_ref):
    u = _rms(x_ref[...], gpre_ref[...])
    p = _dot(u.astype(BF16), win_ref[...])
    tab = tab_ref[...]
    low_half = (_iota((1, LANES), 1) < MLA_ROPE).astype(F32)

    def rope(slab):
        t = slab * tab
        return t + pltpu.roll(t, MLA_ROPE, axis=1)

    cq = _rms(p[:, C_CQ:C_CQ + Q_RANK], gcq_ref[...])
    q = _dot(cq.astype(BF16), wuq_ref[...])
    for h in range(MLA_HEADS):
        base = h * 256
        q_nope = q[:, base:base + MLA_NOPE]
        q_lat = _dot(q_nope.astype(BF16), wuk_ref[h]) * QK_SCALE
        q_rope = rope(q[:, base + MLA_NOPE:base + 256]) * (low_half * QK_SCALE)
        qc_ref[h] = jnp.concatenate([q_lat, q_rope], axis=1).astype(BF16)

    ckv = _rms(p[:, C_CKV:C_CKV + KV_RANK], gckv_ref[...])
    ckv_ref[...] = ckv
    kr = rope(p[:, C_KR:C_KR + LANES])
    kr_ref[...] = kr[:, :MLA_ROPE]
    kc_ref[...] = jnp.concatenate([ckv, kr * low_half], axis=1).astype(BF16)

    rest_ref[:, 0:R_LOGA] = p[:, C_REST:C_A]
    z = _dot(p[:, C_A:D_EXT].astype(BF16), wa2_ref[...]) + ba_ref[...]
    log_sig = jnp.minimum(z, 0.0) - jnp.log(1.0 + jnp.exp(-jnp.abs(z)))
    rest_ref[:, R_LOGA:D_REST] = log_sig * (1.0 / GLA_TAU)


def _inproj(x, tab, w, *, tm):
    n = x.shape[0]
    nt = tab.shape[0] // tm
    full = lambda a: pl.BlockSpec(a.shape, lambda i, nd=a.ndim: (0,) * nd)
    row = lambda width: pl.BlockSpec((tm, width), lambda i: (i, 0))
    head = lambda width: pl.BlockSpec((MLA_HEADS, tm, width), lambda i: (0, i, 0))
    weights = (w["g_pre"], w["w_in"], w["g_cq"], w["w_uq"], w["g_ckv"], w["w_uk"], w["w_a2"], w["b_a"])
    return pl.pallas_call(
        _inproj_body,
        grid=(n // tm,),
        in_specs=[row(D_MODEL), pl.BlockSpec((tm, LANES), lambda i: (i % nt, 0))]
        + [full(a) for a in weights],
        out_specs=[head(D_QK), row(KV_RANK), row(MLA_ROPE), row(D_QK), row(D_REST)],
        out_shape=[
            jax.ShapeDtypeStruct((MLA_HEADS, n, D_QK), BF16),
            jax.ShapeDtypeStruct((n, KV_RANK), F32),
            jax.ShapeDtypeStruct((n, MLA_ROPE), F32),
            jax.ShapeDtypeStruct((n, D_QK), BF16),
            jax.ShapeDtypeStruct((n, D_REST), F32),
        ],
        compiler_params=pltpu.CompilerParams(dimension_semantics=("parallel",),
                                             vmem_limit_bytes=VMEM_LIMIT_BYTES),
        name="inproj",
    )(x, tab, *weights)


def _attn_prompt_body(q_ref, k_ref, wuv_ref, o_ref, s_ref, m_ref, l_ref, acc_ref, *, tq):
    i = pl.program_id(1)
    rows = MLA_HEADS * tq
    m_ref[...] = jnp.full((rows, LANES), NEG_BIG, F32)
    l_ref[...] = jnp.zeros((rows, LANES), F32)
    acc_ref[...] = jnp.zeros((rows, KV_RANK), F32)

    def scores(j, h):
        start = pl.multiple_of(j * tq, tq)
        return _dot_nt(q_ref[h], k_ref[pl.ds(start, tq), :])

    def consume(j, h, s, diagonal):
        start = pl.multiple_of(j * tq, tq)
        rs = slice(h * tq, (h + 1) * tq)
        v = k_ref[pl.ds(start, tq), 0:KV_RANK]
        if diagonal:
            s = jnp.where(_iota((tq, tq), 1) <= _iota((tq, tq), 0), s, NEG_BIG)
        m_old = m_ref[rs]
        m_new = jnp.maximum(m_old, jnp.max(s, axis=1, keepdims=True))
        alpha = jnp.exp2(m_old - m_new)
        p = [jnp.exp2(s[:, c * LANES:(c + 1) * LANES] - m_new) for c in range(tq // LANES)]
        l_new = alpha * l_ref[rs]
        for pc in p:
            l_new = l_new + pc
        l_ref[rs] = l_new
        pv = _dot(jnp.concatenate(p, axis=1).astype(BF16), v)
        acc_ref[rs] = jnp.concatenate([alpha] * (KV_RANK // LANES), axis=1) * acc_ref[rs] + pv
        m_ref[rs] = m_new

    for h in range(MLA_HEADS):
        s_ref[0, h * tq:(h + 1) * tq] = scores(0, h)

    def step(j, slot, last):
        for h in range(MLA_HEADS):
            if not last:
                s_ref[1 - slot, h * tq:(h + 1) * tq] = scores(j + 1, h)
            consume(j, h, s_ref[slot, h * tq:(h + 1) * tq], last)

    def body(j, carry):
        for slot in range(2):
            @pl.when((j & 1) == slot)
            def _():
                step(j, slot, False)
        return carry

    lax.fori_loop(0, i, body, 0)
    for slot in range(2):
        @pl.when((i & 1) == slot)
        def _():
            step(i, slot, True)

    l = jnp.sum(l_ref[...], axis=1, keepdims=True)
    o = acc_ref[...] * (1.0 / l)
    for h in range(MLA_HEADS):
        o_h = o[h * tq:(h + 1) * tq].astype(BF16)
        o_ref[:, h * MLA_V:(h + 1) * MLA_V] = _dot(o_h, wuv_ref[h])


def _attn_prompt(qc, kc, wuv, *, batch, seq, tq):
    n = batch * seq
    nq = seq // tq
    rows = MLA_HEADS * tq
    return pl.pallas_call(
        functools.partial(_attn_prompt_body, tq=tq),
        grid=(batch, nq),
        in_specs=[
            pl.BlockSpec((MLA_HEADS, tq, D_QK), lambda b, i: (0, b * nq + i, 0)),
            pl.BlockSpec((seq, D_QK), lambda b, i: (b, 0)),
            pl.BlockSpec(wuv.shape, lambda b, i: (0, 0, 0)),
        ],
        out_specs=pl.BlockSpec((tq, MLA_HEADS * MLA_V), lambda b, i: (b * nq + i, 0)),
        out_shape=jax.ShapeDtypeStruct((n, MLA_HEADS * MLA_V), F32),
        scratch_shapes=[pltpu.VMEM((2, rows, tq), F32), pltpu.VMEM((rows, LANES), F32),
                        pltpu.VMEM((rows, LANES), F32), pltpu.VMEM((rows, KV_RANK), F32)],
        compiler_params=pltpu.CompilerParams(dimension_semantics=("parallel", "arbitrary"),
                                             vmem_limit_bytes=VMEM_LIMIT_BYTES),
        name="attn_prompt",
    )(qc, kc, wuv)


ROPE_STACK = 2 * LANES // MLA_ROPE


ROPE_SUB = 2 * ROPE_STACK
N_SLOTS = 3


def _attn_sample_body(pt_ref, q_ref, cn_ref, krn_ref, ckv_hbm, kr_hbm, o_ref, kv_buf, kr_buf, sem,
                      *, layer, n_pages, pages, t_new):
    b = pl.program_id(0)
    n_chunks = n_pages // pages
    rows = MLA_HEADS * t_new

    def page_copies(page, i, slot):
        k, p_i = i // ROPE_SUB, i % ROPE_SUB
        g, u = p_i % ROPE_STACK, p_i // ROPE_STACK
        pos = k * ROPE_SUB + g * 2 + u
        return (
            pltpu.make_async_copy(ckv_hbm.at[layer, page],
                                  kv_buf.at[slot, pl.ds(pos * PAGE_SIZE, PAGE_SIZE), :], sem.at[0, slot]),
            pltpu.make_async_copy(kr_hbm.at[layer, page],
                                  kr_buf.at[slot, pl.ds(g * MLA_ROPE, MLA_ROPE),
                                            pl.ds((2 * k + u) * PAGE_SIZE, PAGE_SIZE)], sem.at[1, slot]),
        )

    def start_chunk(seq, c, slot):
        for i in range(pages):
            for cp in page_copies(pt_ref[seq * n_pages + c * pages + i], i, slot):
                cp.start()

    def wait_chunk(slot):
        for i in range(pages):
            for cp in page_copies(0, i, slot):
                cp.wait()

    @pl.when(b == 0)
    def _():
        start_chunk(0, 0, 0)

    ql = q_ref[0, :, 0:KV_RANK]
    qlf = ql.astype(F32)
    qr = q_ref[0, :, KV_RANK:D_QK].astype(F32)
    qr2 = qr + pltpu.roll(qr, MLA_ROPE, axis=1)
    q_wide = jnp.concatenate([qr2, qr2], axis=1)
    bd_shape = (ROPE_STACK * rows, ROPE_STACK * MLA_ROPE)
    on_block = (_iota(bd_shape, 0) >> 4) == (_iota(bd_shape, 1) >> 6)
    q_bd = jnp.where(on_block, jnp.concatenate([q_wide] * ROPE_STACK, axis=0), 0.0).astype(BF16)

    qrf = qr[:, 0:MLA_ROPE]
    cn = cn_ref[0]
    krn = krn_ref[0]
    t_row = _iota((rows, 1), 0) & (t_new - 1)
    scores = []
    for t2 in range(t_new):
        sc = (jnp.sum(qlf * cn[t2:t2 + 1, :], axis=1, keepdims=True)
              + jnp.sum(qrf * krn[t2:t2 + 1, :], axis=1, keepdims=True))
        scores.append(jnp.where(t2 <= t_row, sc, NEG_BIG))
    m = scores[0]
    for sc in scores[1:]:
        m = jnp.maximum(m, sc)
    l = jnp.zeros((rows, 1), F32)
    acc = jnp.zeros((rows, KV_RANK), F32)
    for t2 in range(t_new):
        p = jnp.exp2(scores[t2] - m)
        l = l + p
        acc = acc + p * cn[t2:t2 + 1, :]

    first = b * n_chunks

    def slot_of(c):
        return lax.rem(first + c, N_SLOTS)

    def chunk_scores(c):
        slot = slot_of(c)
        wait_chunk(slot)
        kbs, s_parts = [], []
        for k in range(pages // ROPE_SUB):
            kb = kv_buf[slot, k * ROPE_SUB * PAGE_SIZE:(k + 1) * ROPE_SUB * PAGE_SIZE, :].astype(BF16)
            rb = kr_buf[slot, :, 2 * k * PAGE_SIZE:2 * (k + 1) * PAGE_SIZE].astype(BF16)
            s_rope = _dot(q_bd, rb)
            s_rope = jnp.concatenate([s_rope[g * rows:(g + 1) * rows] for g in range(ROPE_STACK)], axis=1)
            s_parts.append(_dot_nt(ql, kb) + s_rope)
            kbs.append(kb)
        return jnp.concatenate(s_parts, axis=1), kbs

    @pl.when(b == 0)
    def _():
        start_chunk(0, 1, 1)

    nxt = chunk_scores(0)
    for c in range(n_chunks):
        ahead = c + 2
        if ahead < n_chunks:
            start_chunk(b, ahead, slot_of(ahead))
        else:
            @pl.when(b + 1 < pl.num_programs(0))
            def _():
                start_chunk(b + 1, ahead - n_chunks, slot_of(ahead))
        s, kbs = nxt
        if c + 1 < n_chunks:
            nxt = chunk_scores(c + 1)
        m_new = jnp.maximum(m, jnp.max(s, axis=1, keepdims=True))
        alpha = jnp.exp2(m - m_new)
        p = jnp.exp2(s - m_new)
        l = alpha * l + jnp.sum(p, axis=1, keepdims=True)
        acc = alpha * acc
        for k, kb in enumerate(kbs):
            acc = acc + _dot(p[:, k * ROPE_SUB * PAGE_SIZE:(k + 1) * ROPE_SUB * PAGE_SIZE].astype(BF16), kb)
        m = m_new

    o_ref[0] = acc * (1.0 / l)


def _attn_sample(page_table_flat, q, ckv_new, kr_new, cache_ckv, cache_krope_t, *, layer, n_pages, pages):
    bs, rows, _ = q.shape
    t_new = ckv_new.shape[1]
    assert n_pages // pages >= 2 and pages % ROPE_SUB == 0
    grid_spec = pltpu.PrefetchScalarGridSpec(
        num_scalar_prefetch=1,
        grid=(bs,),
        in_specs=[
            pl.BlockSpec((1, rows, D_QK), lambda b, pt: (b, 0, 0)),
            pl.BlockSpec((1, t_new, KV_RANK), lambda b, pt: (b, 0, 0)),
            pl.BlockSpec((1, t_new, MLA_ROPE), lambda b, pt: (b, 0, 0)),
            pl.BlockSpec(memory_space=pl.ANY),
            pl.BlockSpec(memory_space=pl.ANY),
        ],
        out_specs=pl.BlockSpec((1, rows, KV_RANK), lambda b, pt: (b, 0, 0)),
        scratch_shapes=[
            pltpu.VMEM((N_SLOTS, pages * PAGE_SIZE, KV_RANK), F32),
            pltpu.VMEM((N_SLOTS, ROPE_STACK * MLA_ROPE, (pages // ROPE_STACK) * PAGE_SIZE), F32),
            pltpu.SemaphoreType.DMA((2, N_SLOTS)),
        ],
    )
    return pl.pallas_call(
        functools.partial(_attn_sample_body, layer=layer, n_pages=n_pages, pages=pages, t_new=t_new),
        grid_spec=grid_spec,
        out_shape=jax.ShapeDtypeStruct((bs, rows, KV_RANK), F32),
        compiler_params=pltpu.CompilerParams(dimension_semantics=("arbitrary",),
                                             vmem_limit_bytes=VMEM_LIMIT_BYTES),
        name="attn_sample",
    )(page_table_flat, q, ckv_new, kr_new, cache_ckv, cache_krope_t)


def _head_rms(o, g):
    o2 = o * o
    hi = o2.astype(BF16)
    lo = (o2 - hi.astype(F32)).astype(BF16)
    same_head = (_iota((GLA_WIDTH, GLA_WIDTH), 0) >> 6) == (_iota((GLA_WIDTH, GLA_WIDTH), 1) >> 6)
    ones_bd = same_head.astype(BF16)
    ms = (_dot(hi, ones_bd) + _dot(lo, ones_bd)) * (1.0 / GLA_DV)
    return o * lax.rsqrt(ms + NORM_EPS) * g


def _pool_select(s2, s4, s8, s16, count, xp):
    group = _iota((1, POOL_WIDTH), 1) >> 6
    win = jnp.where(group == 0, s2, jnp.where(group == 1, s4, jnp.where(group == 2, s8, s16)))
    return win / count - xp


def _gate_and_project(o_mla, rest, o_pool, o_gla, h, wout_ref, gpost):
    mixed = jnp.concatenate([
        o_mla * _silu(rest[:, R_GMLA:R_GMLA + 512]),
        o_pool * _silu(rest[:, R_GPOOL:R_GPOOL + POOL_WIDTH]),
        o_gla * _silu(rest[:, R_GGLA:R_GGLA + GLA_WIDTH]),
    ], axis=1).astype(BF16)
    y = _dot(mixed, wout_ref[...])
    return h + _rms(y, gpost)


def _mix_prompt_body(rest_ref, omla_ref, h_ref, wpool_ref, pscale_ref, ggla_ref, wout_ref, gpost_ref,
                     hout_ref, sout_ref, hist_ref, st_ref, o_ref, *, tm):
    t = pl.program_id(1)
    n_chunks = tm // GLA_CHUNK

    @pl.when(t == 0)
    def _():
        hist_ref[...] = jnp.zeros(hist_ref.shape, F32)
        st_ref[...] = jnp.zeros(st_ref.shape, F32)

    rest = rest_ref[...]

    xp = rest[:, R_POOL:R_POOL + POOL_WIDTH]
    row = _iota((tm, POOL_WIDTH), 0)

    def shifted(cur, slot, k):
        prev = hist_ref[slot]
        hist_ref[slot] = cur
        return jnp.where(row >= k, pltpu.roll(cur, k, axis=0), pltpu.roll(prev, k, axis=0))

    s2 = xp + shifted(xp, 0, 1)
    s4 = s2 + shifted(s2, 1, 2)
    s8 = s4 + shifted(s4, 2, 4)
    s16 = s8 + shifted(s8, 3, 8)
    window = 2 << (_iota((1, POOL_WIDTH), 1) >> 6)
    count = jnp.minimum(t * tm + row + 1, window).astype(F32)
    pooled = _pool_select(s2, s4, s8, s16, count, xp)
    o_pool = _dot(pooled.astype(BF16), wpool_ref[...]) * pscale_ref[...]

    q = rest[:, R_Q:R_Q + 128]
    k = rest[:, R_K:R_K + 128]
    v = rest[:, R_V:R_V + GLA_WIDTH]
    la = rest[:, R_LOGA:R_LOGA + 128]
    r_i = _iota((tm, tm), 0)
    c_i = _iota((tm, tm), 1)
    same_chunk = (r_i >> 6) == (c_i >> 6)
    tri_bd = (same_chunk & (c_i <= r_i)).astype(BF16)
    ones_bd = same_chunk.astype(BF16)
    la_parts = _split3(la)
    b = _dot3(tri_bd, la_parts)
    b_last = _dot3(ones_bd, la_parts)
    qt = q * GLA_SCALE * jnp.exp(b)
    kt = k * jnp.exp(-b)
    kd = k * jnp.exp(b_last - b)
    dec = jnp.exp(b_last)
    head_mask = ((_iota((4 * GLA_CHUNK, 128), 0) >> 6) == (_iota((4 * GLA_CHUNK, 128), 1) >> 5)).astype(F32)
    causal = (_iota((4 * GLA_CHUNK, GLA_CHUNK), 0) & (GLA_CHUNK - 1)) >= _iota((4 * GLA_CHUNK, GLA_CHUNK), 1)
    lane_head = _iota((GLA_CHUNK, GLA_WIDTH), 1) >> 6
    for n in range(n_chunks):
        lo, hi = n * GLA_CHUNK, (n + 1) * GLA_CHUNK
        q_n = qt[lo:hi]
        v_n = v[lo:hi].astype(BF16)
        q_bd = (jnp.concatenate([q_n] * GLA_HEADS, axis=0) * head_mask).astype(BF16)
        a = _dot_nt(q_bd, kt[lo:hi].astype(BF16))
        a = jnp.where(causal, a, 0.0)
        pv = _dot(a.astype(BF16), v_n)
        o_n = jnp.zeros((GLA_CHUNK, GLA_WIDTH), F32)
        for hh in range(GLA_HEADS):
            o_n = o_n + jnp.where(lane_head == hh, pv[hh * GLA_CHUNK:(hh + 1) * GLA_CHUNK], 0.0)
        st = st_ref[...]
        o_n = o_n + _dot_nt(q_n.astype(BF16), st.astype(BF16))
        upd = _dot_tn(v_n, kd[lo:hi].astype(BF16))
        st_ref[...] = dec[lo:lo + 1, :] * st + upd * head_mask
        o_ref[lo:hi, :] = o_n
    o_gla = _head_rms(o_ref[...], ggla_ref[...])

    hout_ref[...] = _gate_and_project(omla_ref[...], rest, o_pool, o_gla, h_ref[...], wout_ref,
                                      gpost_ref[...])

    @pl.when(t == pl.num_programs(1) - 1)
    def _():
        sout_ref[...] = st_ref[...]


def _mix_prompt(rest, omla, h, w, *, batch, seq, tm):
    n = batch * seq
    nt = seq // tm
    full = lambda a: pl.BlockSpec(a.shape, lambda b, t, nd=a.ndim: (0,) * nd)
    row = lambda width: pl.BlockSpec((tm, width), lambda b, t: (b * nt + t, 0))
    weights = (w["w_pool"], w["pool_scale"], w["g_gla"], w["w_out"], w["g_post"])
    return pl.pallas_call(
        functools.partial(_mix_prompt_body, tm=tm),
        grid=(batch, nt),
        in_specs=[row(D_REST), row(512), row(D_MODEL)] + [full(a) for a in weights],
        out_specs=[row(D_MODEL), pl.BlockSpec((None, 4 * GLA_DV, 128), lambda b, t: (b, 0, 0))],
        out_shape=[jax.ShapeDtypeStruct((n, D_MODEL), F32),
                   jax.ShapeDtypeStruct((batch, 4 * GLA_DV, 128), F32)],
        scratch_shapes=[pltpu.VMEM((4, tm, POOL_WIDTH), F32), pltpu.VMEM((4 * GLA_DV, 128), F32),
                        pltpu.VMEM((tm, GLA_WIDTH), F32)],
        compiler_params=pltpu.CompilerParams(dimension_semantics=("parallel", "arbitrary"),
                                             vmem_limit_bytes=VMEM_LIMIT_BYTES),
        name="mix_prompt",
    )(rest, omla, h, *weights)


def _mix_sample_body(rest_ref, olat_ref, prefix_ref, s0_ref, h_ref, wuv_ref, wpool_ref, pscale_ref,
                     ggla_ref, wout_ref, gpost_ref,
                     hout_ref, snew_ref, st_ref, qT_ref, kT_ref, aT_ref, vT_ref, oT_ref, o_ref,
                     *, t_new, past_len):
    bs = LANES
    n_state_blocks = (GLA_HEADS * GLA_DK * GLA_DV) // LANES
    rest = rest_ref[...]

    o_mla = jnp.concatenate([_dot(olat_ref[h].astype(BF16), wuv_ref[h]) for h in range(MLA_HEADS)], axis=1)

    z = [prefix_ref[j] for j in range(POOL_BUF)]
    z += [rest[t * bs:(t + 1) * bs, R_POOL:R_POOL + POOL_WIDTH] for t in range(t_new)]
    pooled = []
    for t in range(t_new):
        e = POOL_BUF + t
        s2 = z[e] + z[e - 1]
        s4 = s2 + z[e - 2] + z[e - 3]
        s8 = s4 + z[e - 4] + z[e - 5] + z[e - 6] + z[e - 7]
        s16 = s8
        for i in range(8, 16):
            s16 = s16 + z[e - i]
        window = 2 << (_iota((1, POOL_WIDTH), 1) >> 6)
        count = jnp.minimum(past_len + t + 1, window).astype(F32)
        pooled.append(_pool_select(s2, s4, s8, s16, count, z[e]))
    pooled = jnp.concatenate(pooled, axis=0)
    o_pool = _dot(pooled.astype(BF16), wpool_ref[...]) * pscale_ref[...]

    for j in range(n_state_blocks):
        st_ref[j * LANES:(j + 1) * LANES, :] = s0_ref[:, j * LANES:(j + 1) * LANES].T
    for t in range(t_new):
        rows = slice(t * bs, (t + 1) * bs)
        qT_ref[t] = (rest[rows, R_Q:R_Q + 128] * GLA_SCALE).T
        kT_ref[t] = rest[rows, R_K:R_K + 128].T
        aT_ref[t] = jnp.exp(rest[rows, R_LOGA:R_LOGA + 128]).T
        vT_ref[t, 0:LANES] = rest[rows, R_V:R_V + LANES].T
        vT_ref[t, LANES:2 * LANES] = rest[rows, R_V + LANES:R_V + 2 * LANES].T
    for hh in range(GLA_HEADS):
        v_blocks = [vT_ref[t, hh * GLA_DV:(hh + 1) * GLA_DV, :] for t in range(t_new)]

        def body(d, carry, hh=hh, v_blocks=v_blocks):
            r = hh * GLA_DK + d
            base = pl.multiple_of(r * GLA_DV, GLA_DV)
            s = st_ref[pl.ds(base, GLA_DV), :]
            outs = []
            for t in range(t_new):
                s = aT_ref[t, pl.ds(r, 1), :] * s + kT_ref[t, pl.ds(r, 1), :] * v_blocks[t]
                outs.append(carry[t] + qT_ref[t, pl.ds(r, 1), :] * s)
            st_ref[pl.ds(base, GLA_DV), :] = s
            return tuple(outs)

        acc = lax.fori_loop(0, GLA_DK, body, tuple(jnp.zeros((GLA_DV, LANES), F32) for _ in range(t_new)))
        for t in range(t_new):
            oT_ref[t, hh * GLA_DV:(hh + 1) * GLA_DV, :] = acc[t]
    for j in range(n_state_blocks):
        snew_ref[:, j * LANES:(j + 1) * LANES] = st_ref[j * LANES:(j + 1) * LANES, :].T
    for t in range(t_new):
        o_ref[t * bs:(t + 1) * bs, 0:LANES] = oT_ref[t, 0:LANES, :].T
        o_ref[t * bs:(t + 1) * bs, LANES:2 * LANES] = oT_ref[t, LANES:2 * LANES, :].T
    o_gla = _head_rms(o_ref[...], ggla_ref[...])

    hout_ref[...] = _gate_and_project(o_mla, rest, o_pool, o_gla, h_ref[...], wout_ref, gpost_ref[...])


def _mix_sample(rest, olat, prefix_t, s0, h, w, *, t_new, past_len):
    n = rest.shape[0]
    bs = n // t_new
    assert bs == LANES, "sample mixing kernel keeps the sequences on the lane axis"
    state_w = GLA_HEADS * GLA_DK * GLA_DV
    args = (rest, olat, prefix_t, s0, h, w["w_uv"], w["w_pool"], w["pool_scale"], w["g_gla"], w["w_out"],
            w["g_post"])
    full = lambda a: pl.BlockSpec(a.shape, lambda i, nd=a.ndim: (0,) * nd)
    return pl.pallas_call(
        functools.partial(_mix_sample_body, t_new=t_new, past_len=past_len),
        grid=(1,),
        in_specs=[full(a) for a in args],
        out_specs=[pl.BlockSpec((n, D_MODEL), lambda i: (0, 0)), pl.BlockSpec((bs, state_w), lambda i: (0, 0))],
        out_shape=[jax.ShapeDtypeStruct((n, D_MODEL), F32), jax.ShapeDtypeStruct((bs, state_w), F32)],
        scratch_shapes=[
            pltpu.VMEM((state_w, LANES), F32),
            pltpu.VMEM((t_new, 128, LANES), F32), pltpu.VMEM((t_new, 128, LANES), F32),
            pltpu.VMEM((t_new, 128, LANES), F32), pltpu.VMEM((t_new, GLA_WIDTH, LANES), F32),
            pltpu.VMEM((t_new, GLA_WIDTH, LANES), F32), pltpu.VMEM((n, GLA_WIDTH), F32),
        ],
        compiler_params=pltpu.CompilerParams(dimension_semantics=("arbitrary",),
                                             vmem_limit_bytes=VMEM_LIMIT_BYTES),
        name="mix_sample",
    )(*args)


def _rope_table(pos):
    half = MLA_ROPE // 2
    inv = 1.0 / (ROPE_BASE ** (jnp.arange(half, dtype=F32) / half))
    ang = pos.astype(F32)[:, None] * inv[None, :]
    cos, sin = jnp.cos(ang), jnp.sin(ang)
    return jnp.concatenate([cos, cos, -sin, sin], axis=1)


def _layer_weights(l, g_pre, w_in, g_cq, w_uq, g_ckv, w_uk, w_uv, w_pool, pool_scale, w_a2, b_a, g_gla,
                   w_out, g_post):
    half = MLA_ROPE // 2
    wi = w_in[l]
    w_in_ext = jnp.concatenate([
        wi[:, 0:576],
        wi[:, 544:576], wi[:, 512:544],
        wi[:, 576:2112],
        wi[:, 2128:2384],
        wi[:, 2112:2128],
        jnp.zeros((D_MODEL, LANES - GLA_GATE_RANK), F32),
    ], axis=1).astype(BF16)
    wq = w_uq[l]
    rope_cols = wq[:, :, MLA_NOPE:]
    w_uq_ext = jnp.concatenate([wq, rope_cols[:, :, half:], rope_cols[:, :, :half]], axis=2)
    w_uq_ext = w_uq_ext.reshape(Q_RANK, MLA_HEADS * 256).astype(BF16)
    w_pool_bd = jnp.zeros((POOL_WIDTH, POOL_WIDTH), F32)
    for g in range(len(POOL_WINDOWS)):
        w_pool_bd = w_pool_bd.at[g * POOL_GC:(g + 1) * POOL_GC, g * POOL_GC:(g + 1) * POOL_GC].set(w_pool[l, g])
    w_a2_pad = jnp.zeros((LANES, GLA_HEADS * GLA_DK), F32).at[:GLA_GATE_RANK].set(w_a2[l])
    return {
        "g_pre": g_pre[l][None, :], "w_in": w_in_ext, "g_cq": g_cq[l][None, :], "w_uq": w_uq_ext,
        "g_ckv": g_ckv[l][None, :],
        "w_uk": jnp.transpose(w_uk[l], (1, 2, 0)).astype(BF16),
        "w_uv": jnp.transpose(w_uv[l], (1, 0, 2)).astype(BF16),
        "w_pool": w_pool_bd.astype(BF16), "pool_scale": pool_scale[l][None, :],
        "w_a2": w_a2_pad.astype(BF16), "b_a": b_a[l][None, :], "g_gla": g_gla[l][None, :],
        "w_out": w_out[l].astype(BF16), "g_post": g_post[l][None, :],
    }


def _tile_sizes(seq):
    pick = lambda want: max(t for t in (64, 128, 256, 512) if t <= want and seq % t == 0)
    return pick(512), pick(512), pick(256)


@jax.jit
def _forward(x_prompt, x_sample, cache_ckv, cache_krope, state_pool, state_gla, page_table,
             g_pre, w_in, g_cq, w_uq, g_ckv, w_uk, w_uv, w_pool, pool_scale, w_a2, b_a, g_gla, w_out, g_post):
    bp, tp, _ = x_prompt.shape
    bs, ts, _ = x_sample.shape
    depth = w_in.shape[0]
    n_pages = page_table.shape[1]
    past_len = n_pages * PAGE_SIZE
    tm_in, tq, tm_mix = _tile_sizes(tp)
    assert n_pages % (2 * ROPE_SUB) == 0 and MLA_HEADS * ts == 16
    pages_per_step = max(p for p in (8, 16, 32) if n_pages % (2 * p) == 0)
    cache_krope_t = jnp.swapaxes(cache_krope, 2, 3)

    tab_p = _rope_table(jnp.arange(tp))
    tab_s = jnp.repeat(_rope_table(past_len + jnp.arange(ts)), bs, axis=0)
    pt_flat = page_table.reshape(-1).astype(jnp.int32)

    hp = x_prompt.reshape(bp * tp, D_MODEL)
    hs = jnp.transpose(x_sample, (1, 0, 2)).reshape(ts * bs, D_MODEL)
    outs = {k: [] for k in ("ckv_p", "kr_p", "pool_p", "gla_p", "ckv_s", "kr_s", "pool_s", "gla_s")}
    for l in range(depth):
        w = _layer_weights(l, g_pre, w_in, g_cq, w_uq, g_ckv, w_uk, w_uv, w_pool, pool_scale, w_a2, b_a,
                           g_gla, w_out, g_post)
        qc, ckv, kr, kc, rest = _inproj(hp, tab_p, w, tm=tm_in)
        omla = _attn_prompt(qc, kc, w["w_uv"], batch=bp, seq=tp, tq=tq)
        hp, st = _mix_prompt(rest, omla, hp, w, batch=bp, seq=tp, tm=tm_mix)
        outs["ckv_p"].append(ckv.reshape(bp, tp, KV_RANK))
        outs["kr_p"].append(kr.reshape(bp, tp, MLA_ROPE))
        outs["pool_p"].append(rest.reshape(bp, tp, D_REST)[:, tp - POOL_BUF:, R_POOL:R_POOL + POOL_WIDTH])
        st = st.reshape(bp, GLA_HEADS, GLA_DV, GLA_HEADS, GLA_DK)
        st = jnp.stack([st[:, h, :, h, :] for h in range(GLA_HEADS)], axis=1)
        outs["gla_p"].append(jnp.swapaxes(st, 2, 3))
        qc, ckv, kr, _, rest = _inproj(hs, tab_s, w, tm=ts * bs)
        q_b = jnp.transpose(qc.reshape(MLA_HEADS, ts, bs, D_QK), (2, 0, 1, 3)).reshape(bs, MLA_HEADS * ts, D_QK)
        ckv_b = jnp.transpose(ckv.reshape(ts, bs, KV_RANK), (1, 0, 2))
        kr_b = jnp.transpose(kr.reshape(ts, bs, MLA_ROPE), (1, 0, 2))
        olat = _attn_sample(pt_flat, q_b, ckv_b, kr_b, cache_ckv, cache_krope_t, layer=l, n_pages=n_pages,
                            pages=pages_per_step)
        olat = jnp.transpose(olat.reshape(bs, MLA_HEADS, ts, KV_RANK), (1, 2, 0, 3)).reshape(
            MLA_HEADS, ts * bs, KV_RANK)
        prefix_t = jnp.transpose(state_pool[l], (1, 0, 2))
        s0 = state_gla[l].reshape(bs, GLA_HEADS * GLA_DK * GLA_DV)
        hs, s_new = _mix_sample(rest, olat, prefix_t, s0, hs, w, t_new=ts, past_len=past_len)
        xp_b = jnp.transpose(rest[:, R_POOL:R_POOL + POOL_WIDTH].reshape(ts, bs, POOL_WIDTH), (1, 0, 2))
        outs["ckv_s"].append(ckv_b)
        outs["kr_s"].append(kr_b)
        outs["pool_s"].append(jnp.concatenate([state_pool[l], xp_b], axis=1)[:, ts:])
        outs["gla_s"].append(s_new.reshape(bs, GLA_HEADS, GLA_DK, GLA_DV))
    y_prompt = hp.reshape(bp, tp, D_MODEL)
    y_sample = jnp.transpose(hs.reshape(ts, bs, D_MODEL), (1, 0, 2))
    return (y_prompt, y_sample,
            jnp.stack(outs["ckv_p"]), jnp.stack(outs["kr_p"]), jnp.stack(outs["pool_p"]), jnp.stack(outs["gla_p"]),
            jnp.stack(outs["ckv_s"]), jnp.stack(outs["kr_s"]), jnp.stack(outs["pool_s"]), jnp.stack(outs["gla_s"]))


def kernel(x_prompt, x_sample, cache_ckv, cache_krope, state_pool, state_gla, page_table, g_pre, w_in, g_cq,
           w_uq, g_ckv, w_uk, w_uv, w_pool, pool_scale, w_a2, b_a, g_gla, w_out, g_post):
    return _forward(x_prompt, x_sample, cache_ckv, cache_krope, state_pool, state_gla, page_table, g_pre, w_in,
                    g_cq, w_uq, g_ckv, w_uk, w_uv, w_pool, pool_scale, w_a2, b_a, g_gla, w_out, g_post)
```

```python
import functools

import jax
import jax.numpy as jnp
from jax import lax
from jax.experimental import pallas as pl
from jax.experimental.pallas import tpu as pltpu

F32 = jnp.float32
BF16 = jnp.bfloat16

D_MODEL = 1024
MLA_HEADS = 4
MLA_NOPE = 128
MLA_ROPE = 64
MLA_V = 128
Q_RANK = 256
KV_RANK = 256
ROPE_BASE = 10000.0
POOL_WINDOWS = (2, 4, 8, 16)
POOL_WIDTH = 256
POOL_GC = 64
POOL_BUF = 15
GLA_HEADS = 4
GLA_DK = 32
GLA_DV = 64
GLA_WIDTH = 256
GLA_GATE_RANK = 16
GLA_TAU = 16.0
GLA_CHUNK = 64
NORM_EPS = 1e-6
PAGE_SIZE = 128
QK_SCALE = (MLA_NOPE + MLA_ROPE) ** -0.5 * 1.4426950408889634
GLA_SCALE = GLA_DK ** -0.5

LANES = 128
VMEM_LIMIT_BYTES = 56 * 1024 * 1024

C_CQ = 0
C_CKV = 256
C_KR = 512
C_REST = 640
C_A = 2432
D_EXT = 2560
R_GMLA = 0
R_POOL = 512
R_GPOOL = 768
R_Q = 1024
R_K = 1152
R_V = 1280
R_GGLA = 1536
R_LOGA = 1792
D_REST = 1920
D_QK = KV_RANK + LANES

NEG_BIG = -1e30


def _dot(a, b):
    return jnp.dot(a, b, preferred_element_type=F32)


def _dot_nt(a, b):
    return lax.dot_general(a, b, (((1,), (1,)), ((), ())), preferred_element_type=F32)


def _dot_tn(a, b):
    return lax.dot_general(a, b, (((0,), (0,)), ((), ())), preferred_element_type=F32)


def _rms(x, g):
    return x * lax.rsqrt(jnp.mean(x * x, axis=-1, keepdims=True) + NORM_EPS) * g


def _silu(x):
    half = 0.5 * x
    return half + half * jnp.tanh(half)


def _split3(x):
    hi = x.astype(BF16)
    r = x - hi.astype(F32)
    mid = r.astype(BF16)
    lo = (r - mid.astype(F32)).astype(BF16)
    return hi, mid, lo


def _dot3(mat, parts):
    return _dot(mat, parts[0]) + _dot(mat, parts[1]) + _dot(mat, parts[2])


def _iota(shape, dim):
    return lax.broadcasted_iota(jnp.int32, shape, dim)


def _inproj_body(x_ref, tab_ref, gpre_ref, win_ref, gcq_ref, wuq_ref, gckv_ref, wuk_ref,
                 wa2_ref, ba_ref,
                 qc_ref, ckv_ref, kr_ref, kc_ref, rest_ref):
    u = _rms(x_ref[...], gpre_ref[...])
    p = _dot(u.astype(BF16), win_ref[...])
    tab = tab_ref[...]
    low_half = (_iota((1, LANES), 1) < MLA_ROPE).astype(F32)

    def rope(slab):
        t = slab * tab
        return t + pltpu.roll(t, MLA_ROPE, axis=1)

    cq = _rms(p[:, C_CQ:C_CQ + Q_RANK], gcq_ref[...])
    q = _dot(cq.astype(BF16), wuq_ref[...])
    for h in range(MLA_HEADS):
        base = h * 256
        q_nope = q[:, base:base + MLA_NOPE]
        q_lat = _dot(q_nope.astype(BF16), wuk_ref[h]) * QK_SCALE
        q_rope = rope(q[:, base + MLA_NOPE:base + 256]) * (low_half * QK_SCALE)
        qc_ref[h] = jnp.concatenate([q_lat, q_rope], axis=1).astype(BF16)

    ckv = _rms(p[:, C_CKV:C_CKV + KV_RANK], gckv_ref[...])
    ckv_ref[...] = ckv
    kr = rope(p[:, C_KR:C_KR + LANES])
    kr_ref[...] = kr[:, :MLA_ROPE]
    kc_ref[...] = jnp.concatenate([ckv, kr * low_half], axis=1).astype(BF16)

    rest_ref[:, 0:R_LOGA] = p[:, C_REST:C_A]
    z = _dot(p[:, C_A:D_EXT].astype(BF16), wa2_ref[...]) + ba_ref[...]
    log_sig = jnp.minimum(z, 0.0) - jnp.log(1.0 + jnp.exp(-jnp.abs(z)))
    rest_ref[:, R_LOGA:D_REST] = log_sig * (1.0 / GLA_TAU)


def _inproj(x, tab, w, *, tm):
    n = x.shape[0]
    nt = tab.shape[0] // tm
    full = lambda a: pl.BlockSpec(a.shape, lambda i, nd=a.ndim: (0,) * nd)
    row = lambda width: pl.BlockSpec((tm, width), lambda i: (i, 0))
    head = lambda width: pl.BlockSpec((MLA_HEADS, tm, width), lambda i: (0, i, 0))
    weights = (w["g_pre"], w["w_in"], w["g_cq"], w["w_uq"], w["g_ckv"], w["w_uk"], w["w_a2"], w["b_a"])
    return pl.pallas_call(
        _inproj_body,
        grid=(n // tm,),
        in_specs=[row(D_MODEL), pl.BlockSpec((tm, LANES), lambda i: (i % nt, 0))]
        + [full(a) for a in weights],
        out_specs=[head(D_QK), row(KV_RANK), row(MLA_ROPE), row(D_QK), row(D_REST)],
        out_shape=[
            jax.ShapeDtypeStruct((MLA_HEADS, n, D_QK), BF16),
            jax.ShapeDtypeStruct((n, KV_RANK), F32),
            jax.ShapeDtypeStruct((n, MLA_ROPE), F32),
            jax.ShapeDtypeStruct((n, D_QK), BF16),
            jax.ShapeDtypeStruct((n, D_REST), F32),
        ],
        compiler_params=pltpu.CompilerParams(dimension_semantics=("parallel",),
                                             vmem_limit_bytes=VMEM_LIMIT_BYTES),
        name="inproj",
    )(x, tab, *weights)


def _attn_prompt_body(q_ref, k_ref, wuv_ref, o_ref, s_ref, m_ref, l_ref, acc_ref, *, tq):
    i = pl.program_id(1)
    rows = MLA_HEADS * tq
    m_ref[...] = jnp.full((rows, LANES), NEG_BIG, F32)
    l_ref[...] = jnp.zeros((rows, LANES), F32)
    acc_ref[...] = jnp.zeros((rows, KV_RANK), F32)

    def scores(j, h):
        start = pl.multiple_of(j * tq, tq)
        return _dot_nt(q_ref[h], k_ref[pl.ds(start, tq), :])

    def consume(j, h, s, diagonal):
        start = pl.multiple_of(j * tq, tq)
        rs = slice(h * tq, (h + 1) * tq)
        v = k_ref[pl.ds(start, tq), 0:KV_RANK]
        if diagonal:
            s = jnp.where(_iota((tq, tq), 1) <= _iota((tq, tq), 0), s, NEG_BIG)
        m_old = m_ref[rs]
        m_new = jnp.maximum(m_old, jnp.max(s, axis=1, keepdims=True))
        alpha = jnp.exp2(m_old - m_new)
        p = [jnp.exp2(s[:, c * LANES:(c + 1) * LANES] - m_new) for c in range(tq // LANES)]
        l_new = alpha * l_ref[rs]
        for pc in p:
            l_new = l_new + pc
        l_ref[rs] = l_new
        pv = _dot(jnp.concatenate(p, axis=1).astype(BF16), v)
        acc_ref[rs] = jnp.concatenate([alpha] * (KV_RANK // LANES), axis=1) * acc_ref[rs] + pv
        m_ref[rs] = m_new

    for h in range(MLA_HEADS):
        s_ref[0, h * tq:(h + 1) * tq] = scores(0, h)

    def step(j, slot, last):
        for h in range(MLA_HEADS):
            if not last:
                s_ref[1 - slot, h * tq:(h + 1) * tq] = scores(j + 1, h)
            consume(j, h, s_ref[slot, h * tq:(h + 1) * tq], last)

    def body(j, carry):
        for slot in range(2):
            @pl.when((j & 1) == slot)
            def _():
                step(j, slot, False)
        return carry

    lax.fori_loop(0, i, body, 0)
    for slot in range(2):
        @pl.when((i & 1) == slot)
        def _():
            step(i, slot, True)

    l = jnp.sum(l_ref[...], axis=1, keepdims=True)
    o = acc_ref[...] * (1.0 / l)
    for h in range(MLA_HEADS):
        o_h = o[h * tq:(h + 1) * tq].astype(BF16)
        o_ref[:, h * MLA_V:(h + 1) * MLA_V] = _dot(o_h, wuv_ref[h])


def _attn_prompt(qc, kc, wuv, *, batch, seq, tq):
    n = batch * seq
    nq = seq // tq
    rows = MLA_HEADS * tq
    return pl.pallas_call(
        functools.partial(_attn_prompt_body, tq=tq),
        grid=(batch, nq),
        in_specs=[
            pl.BlockSpec((MLA_HEADS, tq, D_QK), lambda b, i: (0, b * nq + i, 0)),
            pl.BlockSpec((seq, D_QK), lambda b, i: (b, 0)),
            pl.BlockSpec(wuv.shape, lambda b, i: (0, 0, 0)),
        ],
        out_specs=pl.BlockSpec((tq, MLA_HEADS * MLA_V), lambda b, i: (b * nq + i, 0)),
        out_shape=jax.ShapeDtypeStruct((n, MLA_HEADS * MLA_V), F32),
        scratch_shapes=[pltpu.VMEM((2, rows, tq), F32), pltpu.VMEM((rows, LANES), F32),
                        pltpu.VMEM((rows, LANES), F32), pltpu.VMEM((rows, KV_RANK), F32)],
        compiler_params=pltpu.CompilerParams(dimension_semantics=("parallel", "arbitrary"),
                                             vmem_limit_bytes=VMEM_LIMIT_BYTES),
        name="attn_prompt",
    )(qc, kc, wuv)


ROPE_STACK = 2 * LANES // MLA_ROPE


ROPE_SUB = 2 * ROPE_STACK
N_SLOTS = 4


def _attn_sample_body(pt_ref, q_ref, cn_ref, krn_ref, ckv_hbm, kr_hbm, o_ref, kv_buf, kr_buf, sem,
                      *, layer, n_pages, pages, t_new):
    b = pl.program_id(0)
    n_chunks = n_pages // pages
    rows = MLA_HEADS * t_new

    def page_copies(page, i, slot):
        k, p_i = i // ROPE_SUB, i % ROPE_SUB
        g, u = p_i % ROPE_STACK, p_i // ROPE_STACK
        pos = k * ROPE_SUB + g * 2 + u
        return (
            pltpu.make_async_copy(ckv_hbm.at[layer, page],
                                  kv_buf.at[slot, pl.ds(pos * PAGE_SIZE, PAGE_SIZE), :], sem.at[0, slot]),
            pltpu.make_async_copy(kr_hbm.at[layer, page],
                                  kr_buf.at[slot, 2 * k + u, pl.ds(g * MLA_ROPE, MLA_ROPE), :], sem.at[1, slot]),
        )

    def start_chunk(seq, c, slot):
        for i in range(pages):
            for cp in page_copies(pt_ref[seq * n_pages + c * pages + i], i, slot):
                cp.start()

    def wait_chunk(slot):
        for i in range(pages):
            for cp in page_copies(0, i, slot):
                cp.wait()

    ql = q_ref[0, :, 0:KV_RANK]
    qlf = ql.astype(F32)
    qr = q_ref[0, :, KV_RANK:D_QK].astype(F32)
    qr2 = qr + pltpu.roll(qr, MLA_ROPE, axis=1)
    q_wide = jnp.concatenate([qr2, qr2], axis=1)
    bd_shape = (ROPE_STACK * rows, ROPE_STACK * MLA_ROPE)
    on_block = (_iota(bd_shape, 0) >> 4) == (_iota(bd_shape, 1) >> 6)
    q_bd = jnp.where(on_block, jnp.concatenate([q_wide] * ROPE_STACK, axis=0), 0.0).astype(BF16)

    qrf = qr[:, 0:MLA_ROPE]
    cn = cn_ref[0]
    krn = krn_ref[0]
    t_row = _iota((rows, 1), 0) & (t_new - 1)
    scores = []
    for t2 in range(t_new):
        sc = (jnp.sum(qlf * cn[t2:t2 + 1, :], axis=1, keepdims=True)
              + jnp.sum(qrf * krn[t2:t2 + 1, :], axis=1, keepdims=True))
        scores.append(jnp.where(t2 <= t_row, sc, NEG_BIG))
    m = scores[0]
    for sc in scores[1:]:
        m = jnp.maximum(m, sc)
    l = jnp.zeros((rows, 1), F32)
    acc = jnp.zeros((rows, KV_RANK), F32)
    for t2 in range(t_new):
        p = jnp.exp2(scores[t2] - m)
        l = l + p
        acc = acc + p * cn[t2:t2 + 1, :]

    first = b * n_chunks

    def slot_of(c):
        if n_chunks % N_SLOTS == 0:
            return c % N_SLOTS
        return lax.rem(first + c, N_SLOTS)

    def chunk_scores(c):
        slot = slot_of(c)
        wait_chunk(slot)
        kbs, s_parts = [], []
        for k in range(pages // ROPE_SUB):
            kb = kv_buf[slot, k * ROPE_SUB * PAGE_SIZE:(k + 1) * ROPE_SUB * PAGE_SIZE, :].astype(BF16)
            rb = jnp.concatenate([kr_buf[slot, 2 * k], kr_buf[slot, 2 * k + 1]], axis=1).astype(BF16)
            s_rope = _dot(q_bd, rb)
            s_rope = jnp.concatenate([s_rope[g * rows:(g + 1) * rows] for g in range(ROPE_STACK)], axis=1)
            s_parts.append(_dot_nt(ql, kb) + s_rope)
            kbs.append(kb)
        return jnp.concatenate(s_parts, axis=1), kbs

    @pl.when(b == 0)
    def _():
        for c0 in range(N_SLOTS - 1):
            start_chunk(c0 // n_chunks, c0 % n_chunks, c0)

    nxt = chunk_scores(0)
    for c in range(n_chunks):
        ahead = c + N_SLOTS - 1
        if ahead < n_chunks:
            start_chunk(b, ahead, slot_of(ahead))
        else:
            @pl.when(b + ahead // n_chunks < pl.num_programs(0))
            def _():
                start_chunk(b + ahead // n_chunks, ahead % n_chunks, slot_of(ahead))
        s, kbs = nxt
        if c + 1 < n_chunks:
            nxt = chunk_scores(c + 1)
        m_new = jnp.maximum(m, jnp.max(s, axis=1, keepdims=True))
        alpha = jnp.exp2(m - m_new)
        p = jnp.exp2(s - m_new)
        l = alpha * l + jnp.sum(p, axis=1, keepdims=True)
        acc = alpha * acc
        for k, kb in enumerate(kbs):
            acc = acc + _dot(p[:, k * ROPE_SUB * PAGE_SIZE:(k + 1) * ROPE_SUB * PAGE_SIZE].astype(BF16), kb)
        m = m_new

    o_ref[0] = acc * (1.0 / l)


def _attn_sample(page_table_flat, q, ckv_new, kr_new, cache_ckv, cache_krope_t, *, layer, n_pages, pages):
    bs, rows, _ = q.shape
    t_new = ckv_new.shape[1]
    assert pages % ROPE_SUB == 0 and bs * (n_pages // pages) >= N_SLOTS
    grid_spec = pltpu.PrefetchScalarGridSpec(
        num_scalar_prefetch=1,
        grid=(bs,),
        in_specs=[
            pl.BlockSpec((1, rows, D_QK), lambda b, pt: (b, 0, 0)),
            pl.BlockSpec((1, t_new, KV_RANK), lambda b, pt: (b, 0, 0)),
            pl.BlockSpec((1, t_new, MLA_ROPE), lambda b, pt: (b, 0, 0)),
            pl.BlockSpec(memory_space=pl.ANY),
            pl.BlockSpec(memory_space=pl.ANY),
        ],
        out_specs=pl.BlockSpec((1, rows, KV_RANK), lambda b, pt: (b, 0, 0)),
        scratch_shapes=[
            pltpu.VMEM((N_SLOTS, pages * PAGE_SIZE, KV_RANK), F32),
            pltpu.VMEM((N_SLOTS, pages // ROPE_STACK, ROPE_STACK * MLA_ROPE, PAGE_SIZE), F32),
            pltpu.SemaphoreType.DMA((2, N_SLOTS)),
        ],
    )
    return pl.pallas_call(
        functools.partial(_attn_sample_body, layer=layer, n_pages=n_pages, pages=pages, t_new=t_new),
        grid_spec=grid_spec,
        out_shape=jax.ShapeDtypeStruct((bs, rows, KV_RANK), F32),
        compiler_params=pltpu.CompilerParams(dimension_semantics=("arbitrary",),
                                             vmem_limit_bytes=VMEM_LIMIT_BYTES),
        name="attn_sample",
    )(page_table_flat, q, ckv_new, kr_new, cache_ckv, cache_krope_t)


def _head_rms(o, g):
    o2 = o * o
    hi = o2.astype(BF16)
    lo = (o2 - hi.astype(F32)).astype(BF16)
    same_head = (_iota((GLA_WIDTH, GLA_WIDTH), 0) >> 6) == (_iota((GLA_WIDTH, GLA_WIDTH), 1) >> 6)
    ones_bd = same_head.astype(BF16)
    ms = (_dot(hi, ones_bd) + _dot(lo, ones_bd)) * (1.0 / GLA_DV)
    return o * lax.rsqrt(ms + NORM_EPS) * g


def _pool_select(s2, s4, s8, s16, count, xp):
    group = _iota((1, POOL_WIDTH), 1) >> 6
    win = jnp.where(group == 0, s2, jnp.where(group == 1, s4, jnp.where(group == 2, s8, s16)))
    return win / count - xp


def _gate_and_project(o_mla, rest, o_pool, o_gla, h, wout_ref, gpost):
    mixed = jnp.concatenate([
        o_mla * _silu(rest[:, R_GMLA:R_GMLA + 512]),
        o_pool * _silu(rest[:, R_GPOOL:R_GPOOL + POOL_WIDTH]),
        o_gla * _silu(rest[:, R_GGLA:R_GGLA + GLA_WIDTH]),
    ], axis=1).astype(BF16)
    y = _dot(mixed, wout_ref[...])
    return h + _rms(y, gpost)


def _mix_prompt_body(rest_ref, omla_ref, h_ref, wpool_ref, pscale_ref, ggla_ref, wout_ref, gpost_ref,
                     hout_ref, sout_ref, hist_ref, st_ref, o_ref, mixed_ref, y_ref, *, tm, nt):
    t = pl.program_id(1)
    n_chunks = tm // GLA_CHUNK
    n_col = D_MODEL // (2 * LANES)

    def mix_stages():
        env = {}

        def pool():
            rest = rest_ref[...]
            xp = rest[:, R_POOL:R_POOL + POOL_WIDTH]
            row = _iota((tm, POOL_WIDTH), 0)

            def shifted(cur, slot, k):
                prev = hist_ref[slot]
                hist_ref[slot] = cur
                return jnp.where(row >= k, pltpu.roll(cur, k, axis=0), pltpu.roll(prev, k, axis=0))

            s2 = xp + shifted(xp, 0, 1)
            s4 = s2 + shifted(s2, 1, 2)
            s8 = s4 + shifted(s4, 2, 4)
            s16 = s8 + shifted(s8, 3, 8)
            window = 2 << (_iota((1, POOL_WIDTH), 1) >> 6)
            count = jnp.minimum(t * tm + row + 1, window).astype(F32)
            pooled = _pool_select(s2, s4, s8, s16, count, xp)
            env["o_pool"] = _dot(pooled.astype(BF16), wpool_ref[...]) * pscale_ref[...]

        def gla_prep():
            rest = rest_ref[...]
            q = rest[:, R_Q:R_Q + 128]
            k = rest[:, R_K:R_K + 128]
            la = rest[:, R_LOGA:R_LOGA + 128]
            r_i = _iota((tm, tm), 0)
            c_i = _iota((tm, tm), 1)
            same_chunk = (r_i >> 6) == (c_i >> 6)
            tri_bd = (same_chunk & (c_i <= r_i)).astype(BF16)
            ones_bd = same_chunk.astype(BF16)
            la_parts = _split3(la)
            b = _dot3(tri_bd, la_parts)
            b_last = _dot3(ones_bd, la_parts)
            env["qt"] = q * GLA_SCALE * jnp.exp(b)
            env["kt"] = k * jnp.exp(-b)
            env["kd"] = k * jnp.exp(b_last - b)
            env["dec"] = jnp.exp(b_last)
            env["v"] = rest[:, R_V:R_V + GLA_WIDTH]

        def gla_chunk(n):
            head_mask = ((_iota((4 * GLA_CHUNK, 128), 0) >> 6) == (_iota((4 * GLA_CHUNK, 128), 1) >> 5)).astype(F32)
            causal = (_iota((4 * GLA_CHUNK, GLA_CHUNK), 0) & (GLA_CHUNK - 1)) >= _iota((4 * GLA_CHUNK, GLA_CHUNK), 1)
            lane_head = _iota((GLA_CHUNK, GLA_WIDTH), 1) >> 6
            lo, hi = n * GLA_CHUNK, (n + 1) * GLA_CHUNK
            q_n = env["qt"][lo:hi]
            v_n = env["v"][lo:hi].astype(BF16)
            q_bd = (jnp.concatenate([q_n] * GLA_HEADS, axis=0) * head_mask).astype(BF16)
            a = _dot_nt(q_bd, env["kt"][lo:hi].astype(BF16))
            a = jnp.where(causal, a, 0.0)
            pv = _dot(a.astype(BF16), v_n)
            o_n = jnp.zeros((GLA_CHUNK, GLA_WIDTH), F32)
            for hh in range(GLA_HEADS):
                o_n = o_n + jnp.where(lane_head == hh, pv[hh * GLA_CHUNK:(hh + 1) * GLA_CHUNK], 0.0)
            st = st_ref[...]
            o_n = o_n + _dot_nt(q_n.astype(BF16), st.astype(BF16))
            upd = _dot_tn(v_n, env["kd"][lo:hi].astype(BF16))
            st_ref[...] = env["dec"][lo:lo + 1, :] * st + upd * head_mask
            o_ref[lo:hi, :] = o_n

        def gate():
            rest = rest_ref[...]
            o_gla = _head_rms(o_ref[...], ggla_ref[...])
            mixed_ref[...] = jnp.concatenate([
                omla_ref[...] * _silu(rest[:, R_GMLA:R_GMLA + 512]),
                env["o_pool"] * _silu(rest[:, R_GPOOL:R_GPOOL + POOL_WIDTH]),
                o_gla * _silu(rest[:, R_GGLA:R_GGLA + GLA_WIDTH]),
            ], axis=1).astype(BF16)
            sout_ref[...] = st_ref[...]

        return [pool, gla_prep] + [functools.partial(gla_chunk, n) for n in range(n_chunks)] + [gate]

    def project_stages():
        def column(n):
            cols = slice(n * 2 * LANES, (n + 1) * 2 * LANES)
            y_ref[:, cols] = _dot(mixed_ref[...], wout_ref[:, cols])

        def finish():
            hout_ref[...] = h_ref[...] + _rms(y_ref[...], gpost_ref[...])

        return [functools.partial(column, n) for n in range(n_col)] + [finish]

    @pl.when(t == 0)
    def _():
        hist_ref[...] = jnp.zeros(hist_ref.shape, F32)
        st_ref[...] = jnp.zeros(st_ref.shape, F32)
        for stage in mix_stages():
            stage()

    @pl.when((t > 0) & (t < nt))
    def _():
        mix, proj = mix_stages(), project_stages()
        order = []
        for i in range(max(len(mix) - 1, len(proj))):
            if i < len(proj):
                order.append(proj[i])
            if i < len(mix) - 1:
                order.append(mix[i])
        order.append(mix[-1])
        for stage in order:
            stage()

    @pl.when(t == nt)
    def _():
        for stage in project_stages():
            stage()


def _mix_prompt(rest, omla, h, w, *, batch, seq, tm):
    n = batch * seq
    nt = seq // tm
    full = lambda a: pl.BlockSpec(a.shape, lambda b, t, nd=a.ndim: (0,) * nd)
    mixed_tile = lambda width: pl.BlockSpec((tm, width), lambda b, t: (b * nt + jnp.minimum(t, nt - 1), 0))
    projected_tile = lambda width: pl.BlockSpec((tm, width), lambda b, t: (b * nt + jnp.maximum(t - 1, 0), 0))
    weights = (w["w_pool"], w["pool_scale"], w["g_gla"], w["w_out"], w["g_post"])
    return pl.pallas_call(
        functools.partial(_mix_prompt_body, tm=tm, nt=nt),
        grid=(batch, nt + 1),
        in_specs=[mixed_tile(D_REST), mixed_tile(512), projected_tile(D_MODEL)] + [full(a) for a in weights],
        out_specs=[projected_tile(D_MODEL), pl.BlockSpec((None, 4 * GLA_DV, 128), lambda b, t: (b, 0, 0))],
        out_shape=[jax.ShapeDtypeStruct((n, D_MODEL), F32),
                   jax.ShapeDtypeStruct((batch, 4 * GLA_DV, 128), F32)],
        scratch_shapes=[pltpu.VMEM((4, tm, POOL_WIDTH), F32), pltpu.VMEM((4 * GLA_DV, 128), F32),
                        pltpu.VMEM((tm, GLA_WIDTH), F32), pltpu.VMEM((tm, D_MODEL), BF16),
                        pltpu.VMEM((tm, D_MODEL), F32)],
        compiler_params=pltpu.CompilerParams(dimension_semantics=("parallel", "arbitrary"),
                                             vmem_limit_bytes=VMEM_LIMIT_BYTES),
        name="mix_prompt",
    )(rest, omla, h, *weights)


def _mix_sample_body(rest_ref, olat_ref, prefix_ref, s0_ref, h_ref, wuv_ref, wpool_ref, pscale_ref,
                     ggla_ref, wout_ref, gpost_ref,
                     hout_ref, snew_ref, st_ref, qT_ref, kT_ref, aT_ref, vT_ref, oT_ref, o_ref,
                     *, t_new, past_len):
    bs = LANES
    n_state_blocks = (GLA_HEADS * GLA_DK * GLA_DV) // LANES
    rest = rest_ref[...]

    o_mla = jnp.concatenate([_dot(olat_ref[h].astype(BF16), wuv_ref[h]) for h in range(MLA_HEADS)], axis=1)

    z = [prefix_ref[j] for j in range(POOL_BUF)]
    z += [rest[t * bs:(t + 1) * bs, R_POOL:R_POOL + POOL_WIDTH] for t in range(t_new)]
    pooled = []
    for t in range(t_new):
        e = POOL_BUF + t
        s2 = z[e] + z[e - 1]
        s4 = s2 + z[e - 2] + z[e - 3]
        s8 = s4 + z[e - 4] + z[e - 5] + z[e - 6] + z[e - 7]
        s16 = s8
        for i in range(8, 16):
            s16 = s16 + z[e - i]
        window = 2 << (_iota((1, POOL_WIDTH), 1) >> 6)
        count = jnp.minimum(past_len + t + 1, window).astype(F32)
        pooled.append(_pool_select(s2, s4, s8, s16, count, z[e]))
    pooled = jnp.concatenate(pooled, axis=0)
    o_pool = _dot(pooled.astype(BF16), wpool_ref[...]) * pscale_ref[...]

    for j in range(n_state_blocks):
        st_ref[j * LANES:(j + 1) * LANES, :] = s0_ref[:, j * LANES:(j + 1) * LANES].T
    for t in range(t_new):
        rows = slice(t * bs, (t + 1) * bs)
        qT_ref[t] = (rest[rows, R_Q:R_Q + 128] * GLA_SCALE).T
        kT_ref[t] = rest[rows, R_K:R_K + 128].T
        aT_ref[t] = jnp.exp(rest[rows, R_LOGA:R_LOGA + 128]).T
        vT_ref[t, 0:LANES] = rest[rows, R_V:R_V + LANES].T
        vT_ref[t, LANES:2 * LANES] = rest[rows, R_V + LANES:R_V + 2 * LANES].T
    for hh in range(GLA_HEADS):
        v_blocks = [vT_ref[t, hh * GLA_DV:(hh + 1) * GLA_DV, :] for t in range(t_new)]

        def body(d, carry, hh=hh, v_blocks=v_blocks):
            r = hh * GLA_DK + d
            base = pl.multiple_of(r * GLA_DV, GLA_DV)
            s = st_ref[pl.ds(base, GLA_DV), :]
            outs = []
            for t in range(t_new):
                s = aT_ref[t, pl.ds(r, 1), :] * s + kT_ref[t, pl.ds(r, 1), :] * v_blocks[t]
                outs.append(carry[t] + qT_ref[t, pl.ds(r, 1), :] * s)
            st_ref[pl.ds(base, GLA_DV), :] = s
            return tuple(outs)

        acc = lax.fori_loop(0, GLA_DK, body, tuple(jnp.zeros((GLA_DV, LANES), F32) for _ in range(t_new)))
        for t in range(t_new):
            oT_ref[t, hh * GLA_DV:(hh + 1) * GLA_DV, :] = acc[t]
    for j in range(n_state_blocks):
        snew_ref[:, j * LANES:(j + 1) * LANES] = st_ref[j * LANES:(j + 1) * LANES, :].T
    for t in range(t_new):
        o_ref[t * bs:(t + 1) * bs, 0:LANES] = oT_ref[t, 0:LANES, :].T
        o_ref[t * bs:(t + 1) * bs, LANES:2 * LANES] = oT_ref[t, LANES:2 * LANES, :].T
    o_gla = _head_rms(o_ref[...], ggla_ref[...])

    hout_ref[...] = _gate_and_project(o_mla, rest, o_pool, o_gla, h_ref[...], wout_ref, gpost_ref[...])


def _mix_sample(rest, olat, prefix_t, s0, h, w, *, t_new, past_len):
    n = rest.shape[0]
    bs = n // t_new
    assert bs == LANES, "sample mixing kernel keeps the sequences on the lane axis"
    state_w = GLA_HEADS * GLA_DK * GLA_DV
    args = (rest, olat, prefix_t, s0, h, w["w_uv"], w["w_pool"], w["pool_scale"], w["g_gla"], w["w_out"],
            w["g_post"])
    full = lambda a: pl.BlockSpec(a.shape, lambda i, nd=a.ndim: (0,) * nd)
    return pl.pallas_call(
        functools.partial(_mix_sample_body, t_new=t_new, past_len=past_len),
        grid=(1,),
        in_specs=[full(a) for a in args],
        out_specs=[pl.BlockSpec((n, D_MODEL), lambda i: (0, 0)), pl.BlockSpec((bs, state_w), lambda i: (0, 0))],
        out_shape=[jax.ShapeDtypeStruct((n, D_MODEL), F32), jax.ShapeDtypeStruct((bs, state_w), F32)],
        scratch_shapes=[
            pltpu.VMEM((state_w, LANES), F32),
            pltpu.VMEM((t_new, 128, LANES), F32), pltpu.VMEM((t_new, 128, LANES), F32),
            pltpu.VMEM((t_new, 128, LANES), F32), pltpu.VMEM((t_new, GLA_WIDTH, LANES), F32),
            pltpu.VMEM((t_new, GLA_WIDTH, LANES), F32), pltpu.VMEM((n, GLA_WIDTH), F32),
        ],
        compiler_params=pltpu.CompilerParams(dimension_semantics=("arbitrary",),
                                             vmem_limit_bytes=VMEM_LIMIT_BYTES),
        name="mix_sample",
    )(*args)


def _rope_table(pos):
    half = MLA_ROPE // 2
    inv = 1.0 / (ROPE_BASE ** (jnp.arange(half, dtype=F32) / half))
    ang = pos.astype(F32)[:, None] * inv[None, :]
    cos, sin = jnp.cos(ang), jnp.sin(ang)
    return jnp.concatenate([cos, cos, -sin, sin], axis=1)


def _layer_weights(l, g_pre, w_in, g_cq, w_uq, g_ckv, w_uk, w_uv, w_pool, pool_scale, w_a2, b_a, g_gla,
                   w_out, g_post):
    half = MLA_ROPE // 2
    wi = w_in[l]
    w_in_ext = jnp.concatenate([
        wi[:, 0:576],
        wi[:, 544:576], wi[:, 512:544],
        wi[:, 576:2112],
        wi[:, 2128:2384],
        wi[:, 2112:2128],
        jnp.zeros((D_MODEL, LANES - GLA_GATE_RANK), F32),
    ], axis=1).astype(BF16)
    wq = w_uq[l]
    rope_cols = wq[:, :, MLA_NOPE:]
    w_uq_ext = jnp.concatenate([wq, rope_cols[:, :, half:], rope_cols[:, :, :half]], axis=2)
    w_uq_ext = w_uq_ext.reshape(Q_RANK, MLA_HEADS * 256).astype(BF16)
    w_pool_bd = jnp.zeros((POOL_WIDTH, POOL_WIDTH), F32)
    for g in range(len(POOL_WINDOWS)):
        w_pool_bd = w_pool_bd.at[g * POOL_GC:(g + 1) * POOL_GC, g * POOL_GC:(g + 1) * POOL_GC].set(w_pool[l, g])
    w_a2_pad = jnp.zeros((LANES, GLA_HEADS * GLA_DK), F32).at[:GLA_GATE_RANK].set(w_a2[l])
    return {
        "g_pre": g_pre[l][None, :], "w_in": w_in_ext, "g_cq": g_cq[l][None, :], "w_uq": w_uq_ext,
        "g_ckv": g_ckv[l][None, :],
        "w_uk": jnp.transpose(w_uk[l], (1, 2, 0)).astype(BF16),
        "w_uv": jnp.transpose(w_uv[l], (1, 0, 2)).astype(BF16),
        "w_pool": w_pool_bd.astype(BF16), "pool_scale": pool_scale[l][None, :],
        "w_a2": w_a2_pad.astype(BF16), "b_a": b_a[l][None, :], "g_gla": g_gla[l][None, :],
        "w_out": w_out[l].astype(BF16), "g_post": g_post[l][None, :],
    }


def _tile_sizes(seq):
    pick = lambda want: max(t for t in (64, 128, 256, 512) if t <= want and seq % t == 0)
    return pick(512), pick(512), pick(256)


@jax.jit
def _forward(x_prompt, x_sample, cache_ckv, cache_krope, state_pool, state_gla, page_table,
             g_pre, w_in, g_cq, w_uq, g_ckv, w_uk, w_uv, w_pool, pool_scale, w_a2, b_a, g_gla, w_out, g_post):
    bp, tp, _ = x_prompt.shape
    bs, ts, _ = x_sample.shape
    depth = w_in.shape[0]
    n_pages = page_table.shape[1]
    past_len = n_pages * PAGE_SIZE
    tm_in, tq, tm_mix = _tile_sizes(tp)
    assert n_pages % (2 * ROPE_SUB) == 0 and MLA_HEADS * ts == 16
    pages_per_step = max(p for p in (8, 16, 32) if n_pages % (2 * p) == 0)
    cache_krope_t = jnp.swapaxes(cache_krope, 2, 3)

    tab_p = _rope_table(jnp.arange(tp))
    tab_s = jnp.repeat(_rope_table(past_len + jnp.arange(ts)), bs, axis=0)
    pt_flat = page_table.reshape(-1).astype(jnp.int32)

    hp = x_prompt.reshape(bp * tp, D_MODEL)
    hs = jnp.transpose(x_sample, (1, 0, 2)).reshape(ts * bs, D_MODEL)
    outs = {k: [] for k in ("ckv_p", "kr_p", "pool_p", "gla_p", "ckv_s", "kr_s", "pool_s", "gla_s")}
    for l in range(depth):
        w = _layer_weights(l, g_pre, w_in, g_cq, w_uq, g_ckv, w_uk, w_uv, w_pool, pool_scale, w_a2, b_a,
                           g_gla, w_out, g_post)
        qc, ckv, kr, kc, rest = _inproj(hp, tab_p, w, tm=tm_in)
        omla = _attn_prompt(qc, kc, w["w_uv"], batch=bp, seq=tp, tq=tq)
        hp, st = _mix_prompt(rest, omla, hp, w, batch=bp, seq=tp, tm=tm_mix)
        outs["ckv_p"].append(ckv.reshape(bp, tp, KV_RANK))
        outs["kr_p"].append(kr.reshape(bp, tp, MLA_ROPE))
        outs["pool_p"].append(rest.reshape(bp, tp, D_REST)[:, tp - POOL_BUF:, R_POOL:R_POOL + POOL_WIDTH])
        st = st.reshape(bp, GLA_HEADS, GLA_DV, GLA_HEADS, GLA_DK)
        st = jnp.stack([st[:, h, :, h, :] for h in range(GLA_HEADS)], axis=1)
        outs["gla_p"].append(jnp.swapaxes(st, 2, 3))
        qc, ckv, kr, _, rest = _inproj(hs, tab_s, w, tm=ts * bs)
        q_b = jnp.transpose(qc.reshape(MLA_HEADS, ts, bs, D_QK), (2, 0, 1, 3)).reshape(bs, MLA_HEADS * ts, D_QK)
        ckv_b = jnp.transpose(ckv.reshape(ts, bs, KV_RANK), (1, 0, 2))
        kr_b = jnp.transpose(kr.reshape(ts, bs, MLA_ROPE), (1, 0, 2))
        olat = _attn_sample(pt_flat, q_b, ckv_b, kr_b, cache_ckv, cache_krope_t, layer=l, n_pages=n_pages,
                            pages=pages_per_step)
        olat = jnp.transpose(olat.reshape(bs, MLA_HEADS, ts, KV_RANK), (1, 2, 0, 3)).reshape(
            MLA_HEADS, ts * bs, KV_RANK)
        prefix_t = jnp.transpose(state_pool[l], (1, 0, 2))
        s0 = state_gla[l].reshape(bs, GLA_HEADS * GLA_DK * GLA_DV)
        hs, s_new = _mix_sample(rest, olat, prefix_t, s0, hs, w, t_new=ts, past_len=past_len)
        xp_b = jnp.transpose(rest[:, R_POOL:R_POOL + POOL_WIDTH].reshape(ts, bs, POOL_WIDTH), (1, 0, 2))
        outs["ckv_s"].append(ckv_b)
        outs["kr_s"].append(kr_b)
        outs["pool_s"].append(jnp.concatenate([state_pool[l], xp_b], axis=1)[:, ts:])
        outs["gla_s"].append(s_new.reshape(bs, GLA_HEADS, GLA_DK, GLA_DV))
    y_prompt = hp.reshape(bp, tp, D_MODEL)
    y_sample = jnp.transpose(hs.reshape(ts, bs, D_MODEL), (1, 0, 2))
    return (y_prompt, y_sample,
            jnp.stack(outs["ckv_p"]), jnp.stack(outs["kr_p"]), jnp.stack(outs["pool_p"]), jnp.stack(outs["gla_p"]),
            jnp.stack(outs["ckv_s"]), jnp.stack(outs["kr_s"]), jnp.stack(outs["pool_s"]), jnp.stack(outs["gla_s"]))


def kernel(x_prompt, x_sample, cache_ckv, cache_krope, state_pool, state_gla, page_table, g_pre, w_in, g_cq,
           w_uq, g_ckv, w_uk, w_uv, w_pool, pool_scale, w_a2, b_a, g_gla, w_out, g_post):
    return _forward(x_prompt, x_sample, cache_ckv, cache_krope, state_pool, state_gla, page_table, g_pre, w_in,
                    g_cq, w_uq, g_ckv, w_uk, w_uv, w_pool, pool_scale, w_a2, b_a, g_gla, w_out, g_post)
```

```python
import functools

import jax
import jax.numpy as jnp
from jax import lax
from jax.experimental import pallas as pl
from jax.experimental.pallas import tpu as pltpu

F32 = jnp.float32
BF16 = jnp.bfloat16

D_MODEL = 1024
MLA_HEADS = 4
MLA_NOPE = 128
MLA_ROPE = 64
MLA_V = 128
Q_RANK = 256
KV_RANK = 256
ROPE_BASE = 10000.0
POOL_WINDOWS = (2, 4, 8, 16)
POOL_WIDTH = 256
POOL_GC = 64
POOL_BUF = 15
GLA_HEADS = 4
GLA_DK = 32
GLA_DV = 64
GLA_WIDTH = 256
GLA_GATE_RANK = 16
GLA_TAU = 16.0
GLA_CHUNK = 64
CUMSUM_ROWS = 4 * GLA_CHUNK
NORM_EPS = 1e-6
PAGE_SIZE = 128
QK_SCALE = (MLA_NOPE + MLA_ROPE) ** -0.5 * 1.4426950408889634
GLA_SCALE = GLA_DK ** -0.5

LANES = 128
VMEM_LIMIT_BYTES = 56 * 1024 * 1024

C_CQ = 0
C_CKV = 256
C_KR = 512
C_REST = 640
C_A = 2432
D_EXT = 2560
R_GMLA = 0
R_POOL = 512
R_GPOOL = 768
R_Q = 1024
R_K = 1152
R_V = 1280
R_GGLA = 1536
R_LOGA = 1792
D_REST = 1920
D_QK = KV_RANK + LANES

NEG_BIG = -1e30


def _dot(a, b):
    return jnp.dot(a, b, preferred_element_type=F32)


def _dot_nt(a, b):
    return lax.dot_general(a, b, (((1,), (1,)), ((), ())), preferred_element_type=F32)


def _dot_tn(a, b):
    return lax.dot_general(a, b, (((0,), (0,)), ((), ())), preferred_element_type=F32)


def _rms(x, g):
    return x * lax.rsqrt(jnp.mean(x * x, axis=-1, keepdims=True) + NORM_EPS) * g


def _silu(x):
    half = 0.5 * x
    return half + half * jnp.tanh(half)


def _split3(x):
    hi = x.astype(BF16)
    r = x - hi.astype(F32)
    mid = r.astype(BF16)
    lo = (r - mid.astype(F32)).astype(BF16)
    return hi, mid, lo


def _dot3(mat, parts):
    return _dot(mat, parts[0]) + _dot(mat, parts[1]) + _dot(mat, parts[2])


def _iota(shape, dim):
    return lax.broadcasted_iota(jnp.int32, shape, dim)


def _inproj_body(x_ref, tab_ref, gpre_ref, win_ref, gcq_ref, wuq_ref, gckv_ref, wuk_ref,
                 wa2_ref, ba_ref,
                 qc_ref, ckv_ref, kr_ref, kc_ref, rest_ref):
    u = _rms(x_ref[...], gpre_ref[...])
    p = _dot(u.astype(BF16), win_ref[...])
    tab = tab_ref[...]
    low_half = (_iota((1, LANES), 1) < MLA_ROPE).astype(F32)

    def rope(slab):
        t = slab * tab
        return t + pltpu.roll(t, MLA_ROPE, axis=1)

    cq = _rms(p[:, C_CQ:C_CQ + Q_RANK], gcq_ref[...])
    q = _dot(cq.astype(BF16), wuq_ref[...])
    for h in range(MLA_HEADS):
        base = h * 256
        q_nope = q[:, base:base + MLA_NOPE]
        q_lat = _dot(q_nope.astype(BF16), wuk_ref[h]) * QK_SCALE
        q_rope = rope(q[:, base + MLA_NOPE:base + 256]) * (low_half * QK_SCALE)
        qc_ref[h] = jnp.concatenate([q_lat, q_rope], axis=1).astype(BF16)

    ckv = _rms(p[:, C_CKV:C_CKV + KV_RANK], gckv_ref[...])
    ckv_ref[...] = ckv
    kr = rope(p[:, C_KR:C_KR + LANES])
    kr_ref[...] = kr[:, :MLA_ROPE]
    kc_ref[...] = jnp.concatenate([ckv, kr * low_half], axis=1).astype(BF16)

    rest_ref[:, 0:R_LOGA] = p[:, C_REST:C_A]
    z = _dot(p[:, C_A:D_EXT].astype(BF16), wa2_ref[...]) + ba_ref[...]
    log_sig = jnp.minimum(z, 0.0) - jnp.log(1.0 + jnp.exp(-jnp.abs(z)))
    rest_ref[:, R_LOGA:D_REST] = log_sig * (1.0 / GLA_TAU)


def _inproj(x, tab, w, *, tm):
    n = x.shape[0]
    nt = tab.shape[0] // tm
    full = lambda a: pl.BlockSpec(a.shape, lambda i, nd=a.ndim: (0,) * nd)
    row = lambda width: pl.BlockSpec((tm, width), lambda i: (i, 0))
    head = lambda width: pl.BlockSpec((MLA_HEADS, tm, width), lambda i: (0, i, 0))
    weights = (w["g_pre"], w["w_in"], w["g_cq"], w["w_uq"], w["g_ckv"], w["w_uk"], w["w_a2"], w["b_a"])
    return pl.pallas_call(
        _inproj_body,
        grid=(n // tm,),
        in_specs=[row(D_MODEL), pl.BlockSpec((tm, LANES), lambda i: (i % nt, 0))]
        + [full(a) for a in weights],
        out_specs=[head(D_QK), row(KV_RANK), row(MLA_ROPE), row(D_QK), row(D_REST)],
        out_shape=[
            jax.ShapeDtypeStruct((MLA_HEADS, n, D_QK), BF16),
            jax.ShapeDtypeStruct((n, KV_RANK), F32),
            jax.ShapeDtypeStruct((n, MLA_ROPE), F32),
            jax.ShapeDtypeStruct((n, D_QK), BF16),
            jax.ShapeDtypeStruct((n, D_REST), F32),
        ],
        compiler_params=pltpu.CompilerParams(dimension_semantics=("parallel",),
                                             vmem_limit_bytes=VMEM_LIMIT_BYTES),
        name="inproj",
    )(x, tab, *weights)


def _attn_prompt_body(q_ref, k_ref, wuv_ref, o_ref, s_ref, m_ref, l_ref, acc_ref, *, tq):
    i = pl.program_id(1)
    rows = MLA_HEADS * tq
    m_ref[...] = jnp.full((rows, LANES), NEG_BIG, F32)
    l_ref[...] = jnp.zeros((rows, LANES), F32)
    acc_ref[...] = jnp.zeros((rows, KV_RANK), F32)

    def scores(j, h):
        start = pl.multiple_of(j * tq, tq)
        return _dot_nt(q_ref[h], k_ref[pl.ds(start, tq), :])

    def consume(j, h, s, diagonal):
        start = pl.multiple_of(j * tq, tq)
        rs = slice(h * tq, (h + 1) * tq)
        v = k_ref[pl.ds(start, tq), 0:KV_RANK]
        if diagonal:
            s = jnp.where(_iota((tq, tq), 1) <= _iota((tq, tq), 0), s, NEG_BIG)
        m_old = m_ref[rs]
        m_new = jnp.maximum(m_old, jnp.max(s, axis=1, keepdims=True))
        alpha = jnp.exp2(m_old - m_new)
        p = [jnp.exp2(s[:, c * LANES:(c + 1) * LANES] - m_new) for c in range(tq // LANES)]
        l_new = alpha * l_ref[rs]
        for pc in p:
            l_new = l_new + pc
        l_ref[rs] = l_new
        pv = _dot(jnp.concatenate(p, axis=1).astype(BF16), v)
        acc_ref[rs] = jnp.concatenate([alpha] * (KV_RANK // LANES), axis=1) * acc_ref[rs] + pv
        m_ref[rs] = m_new

    for h in range(MLA_HEADS):
        s_ref[0, h * tq:(h + 1) * tq] = scores(0, h)

    def step(j, slot, last):
        for h in range(MLA_HEADS):
            if not last:
                s_ref[1 - slot, h * tq:(h + 1) * tq] = scores(j + 1, h)
            consume(j, h, s_ref[slot, h * tq:(h + 1) * tq], last)

    def body(j, carry):
        for slot in range(2):
            @pl.when((j & 1) == slot)
            def _():
                step(j, slot, False)
        return carry

    lax.fori_loop(0, i, body, 0)
    for slot in range(2):
        @pl.when((i & 1) == slot)
        def _():
            step(i, slot, True)

    l = jnp.sum(l_ref[...], axis=1, keepdims=True)
    o = acc_ref[...] * (1.0 / l)
    for h in range(MLA_HEADS):
        o_h = o[h * tq:(h + 1) * tq].astype(BF16)
        o_ref[:, h * MLA_V:(h + 1) * MLA_V] = _dot(o_h, wuv_ref[h])


def _attn_prompt(qc, kc, wuv, *, batch, seq, tq):
    n = batch * seq
    nq = seq // tq
    rows = MLA_HEADS * tq
    return pl.pallas_call(
        functools.partial(_attn_prompt_body, tq=tq),
        grid=(batch, nq),
        in_specs=[
            pl.BlockSpec((MLA_HEADS, tq, D_QK), lambda b, i: (0, b * nq + i, 0)),
            pl.BlockSpec((seq, D_QK), lambda b, i: (b, 0)),
            pl.BlockSpec(wuv.shape, lambda b, i: (0, 0, 0)),
        ],
        out_specs=pl.BlockSpec((tq, MLA_HEADS * MLA_V), lambda b, i: (b * nq + i, 0)),
        out_shape=jax.ShapeDtypeStruct((n, MLA_HEADS * MLA_V), F32),
        scratch_shapes=[pltpu.VMEM((2, rows, tq), F32), pltpu.VMEM((rows, LANES), F32),
                        pltpu.VMEM((rows, LANES), F32), pltpu.VMEM((rows, KV_RANK), F32)],
        compiler_params=pltpu.CompilerParams(dimension_semantics=("parallel", "arbitrary"),
                                             vmem_limit_bytes=VMEM_LIMIT_BYTES),
        name="attn_prompt",
    )(qc, kc, wuv)


ROPE_STACK = 2 * LANES // MLA_ROPE
ROPE_SUB = 2 * ROPE_STACK
N_SLOTS = 4


def _attn_sample_body(pt_ref, q_ref, cn_ref, krn_ref, ckv_hbm, kr_hbm, o_ref, kv_buf, kr_buf, sem,
                      *, layer, n_pages, pages, t_new):
    b = pl.program_id(0)
    n_chunks = n_pages // pages
    rows = MLA_HEADS * t_new

    def page_copies(page, i, slot):
        k, p_i = i // ROPE_SUB, i % ROPE_SUB
        g, u = p_i % ROPE_STACK, p_i // ROPE_STACK
        pos = k * ROPE_SUB + g * 2 + u
        return (
            pltpu.make_async_copy(ckv_hbm.at[layer, page],
                                  kv_buf.at[slot, pl.ds(pos * PAGE_SIZE, PAGE_SIZE), :], sem.at[0, slot]),
            pltpu.make_async_copy(kr_hbm.at[layer, page],
                                  kr_buf.at[slot, 2 * k + u, pl.ds(g * MLA_ROPE, MLA_ROPE), :], sem.at[1, slot]),
        )

    def start_chunk(seq, c, slot):
        for i in range(pages):
            for cp in page_copies(pt_ref[seq * n_pages + c * pages + i], i, slot):
                cp.start()

    def wait_chunk(slot):
        for i in range(pages):
            for cp in page_copies(0, i, slot):
                cp.wait()

    ql = q_ref[0, :, 0:KV_RANK]
    qlf = ql.astype(F32)
    qr = q_ref[0, :, KV_RANK:D_QK].astype(F32)
    qr2 = qr + pltpu.roll(qr, MLA_ROPE, axis=1)
    q_wide = jnp.concatenate([qr2, qr2], axis=1)
    bd_shape = (ROPE_STACK * rows, ROPE_STACK * MLA_ROPE)
    on_block = (_iota(bd_shape, 0) >> 4) == (_iota(bd_shape, 1) >> 6)
    q_bd = jnp.where(on_block, jnp.concatenate([q_wide] * ROPE_STACK, axis=0), 0.0).astype(BF16)

    qrf = qr[:, 0:MLA_ROPE]
    cn = cn_ref[0]
    krn = krn_ref[0]
    t_row = _iota((rows, 1), 0) & (t_new - 1)
    scores = []
    for t2 in range(t_new):
        sc = (jnp.sum(qlf * cn[t2:t2 + 1, :], axis=1, keepdims=True)
              + jnp.sum(qrf * krn[t2:t2 + 1, :], axis=1, keepdims=True))
        scores.append(jnp.where(t2 <= t_row, sc, NEG_BIG))
    m = scores[0]
    for sc in scores[1:]:
        m = jnp.maximum(m, sc)
    l = jnp.zeros((rows, 1), F32)
    acc = jnp.zeros((rows, KV_RANK), F32)
    for t2 in range(t_new):
        p = jnp.exp2(scores[t2] - m)
        l = l + p
        acc = acc + p * cn[t2:t2 + 1, :]

    first = b * n_chunks

    def slot_of(c):
        if n_chunks % N_SLOTS == 0:
            return c % N_SLOTS
        return lax.rem(first + c, N_SLOTS)

    def chunk_scores(c):
        slot = slot_of(c)
        wait_chunk(slot)
        kbs, s_parts = [], []
        for k in range(pages // ROPE_SUB):
            kb = kv_buf[slot, k * ROPE_SUB * PAGE_SIZE:(k + 1) * ROPE_SUB * PAGE_SIZE, :].astype(BF16)
            rb = jnp.concatenate([kr_buf[slot, 2 * k], kr_buf[slot, 2 * k + 1]], axis=1).astype(BF16)
            s_rope = _dot(q_bd, rb)
            s_rope = jnp.concatenate([s_rope[g * rows:(g + 1) * rows] for g in range(ROPE_STACK)], axis=1)
            s_parts.append(_dot_nt(ql, kb) + s_rope)
            kbs.append(kb)
        return jnp.concatenate(s_parts, axis=1), kbs

    @pl.when(b == 0)
    def _():
        for c0 in range(N_SLOTS - 1):
            start_chunk(c0 // n_chunks, c0 % n_chunks, c0)

    nxt = chunk_scores(0)
    for c in range(n_chunks):
        ahead = c + N_SLOTS - 1
        if ahead < n_chunks:
            start_chunk(b, ahead, slot_of(ahead))
        else:
            @pl.when(b + ahead // n_chunks < pl.num_programs(0))
            def _():
                start_chunk(b + ahead // n_chunks, ahead % n_chunks, slot_of(ahead))
        s, kbs = nxt
        if c + 1 < n_chunks:
            nxt = chunk_scores(c + 1)
        m_new = jnp.maximum(m, jnp.max(s, axis=1, keepdims=True))
        alpha = jnp.exp2(m - m_new)
        p = jnp.exp2(s - m_new)
        l = alpha * l + jnp.sum(p, axis=1, keepdims=True)
        acc = alpha * acc
        for k, kb in enumerate(kbs):
            acc = acc + _dot(p[:, k * ROPE_SUB * PAGE_SIZE:(k + 1) * ROPE_SUB * PAGE_SIZE].astype(BF16), kb)
        m = m_new

    o_ref[0] = acc * (1.0 / l)


def _attn_sample(page_table_flat, q, ckv_new, kr_new, cache_ckv, cache_krope_t, *, layer, n_pages, pages):
    bs, rows, _ = q.shape
    t_new = ckv_new.shape[1]
    assert pages % ROPE_SUB == 0 and bs * (n_pages // pages) >= N_SLOTS
    grid_spec = pltpu.PrefetchScalarGridSpec(
        num_scalar_prefetch=1,
        grid=(bs,),
        in_specs=[
            pl.BlockSpec((1, rows, D_QK), lambda b, pt: (b, 0, 0)),
            pl.BlockSpec((1, t_new, KV_RANK), lambda b, pt: (b, 0, 0)),
            pl.BlockSpec((1, t_new, MLA_ROPE), lambda b, pt: (b, 0, 0)),
            pl.BlockSpec(memory_space=pl.ANY),
            pl.BlockSpec(memory_space=pl.ANY),
        ],
        out_specs=pl.BlockSpec((1, rows, KV_RANK), lambda b, pt: (b, 0, 0)),
        scratch_shapes=[
            pltpu.VMEM((N_SLOTS, pages * PAGE_SIZE, KV_RANK), F32),
            pltpu.VMEM((N_SLOTS, pages // ROPE_STACK, ROPE_STACK * MLA_ROPE, PAGE_SIZE), F32),
            pltpu.SemaphoreType.DMA((2, N_SLOTS)),
        ],
    )
    return pl.pallas_call(
        functools.partial(_attn_sample_body, layer=layer, n_pages=n_pages, pages=pages, t_new=t_new),
        grid_spec=grid_spec,
        out_shape=jax.ShapeDtypeStruct((bs, rows, KV_RANK), F32),
        compiler_params=pltpu.CompilerParams(dimension_semantics=("arbitrary",),
                                             vmem_limit_bytes=VMEM_LIMIT_BYTES),
        name="attn_sample",
    )(page_table_flat, q, ckv_new, kr_new, cache_ckv, cache_krope_t)


def _head_rms(o, g):
    o2 = o * o
    hi = o2.astype(BF16)
    lo = (o2 - hi.astype(F32)).astype(BF16)
    same_head = (_iota((GLA_WIDTH, GLA_WIDTH), 0) >> 6) == (_iota((GLA_WIDTH, GLA_WIDTH), 1) >> 6)
    ones_bd = same_head.astype(BF16)
    ms = (_dot(hi, ones_bd) + _dot(lo, ones_bd)) * (1.0 / GLA_DV)
    return o * lax.rsqrt(ms + NORM_EPS) * g


def _pool_select(s2, s4, s8, s16, count, xp):
    group = _iota((1, POOL_WIDTH), 1) >> 6
    win = jnp.where(group == 0, s2, jnp.where(group == 1, s4, jnp.where(group == 2, s8, s16)))
    return win / count - xp


def _gate_and_project(o_mla, rest, o_pool, o_gla, h, wout_ref, gpost):
    mixed = jnp.concatenate([
        o_mla * _silu(rest[:, R_GMLA:R_GMLA + 512]),
        o_pool * _silu(rest[:, R_GPOOL:R_GPOOL + POOL_WIDTH]),
        o_gla * _silu(rest[:, R_GGLA:R_GGLA + GLA_WIDTH]),
    ], axis=1).astype(BF16)
    y = _dot(mixed, wout_ref[...])
    return h + _rms(y, gpost)


def _mix_prompt_body(rest_ref, omla_ref, h_ref, wpool_ref, pscale_ref, ggla_ref, wout_ref, gpost_ref,
                     hout_ref, sout_ref, hist_ref, st_ref, o_ref, *, tm):
    t = pl.program_id(1)
    n_chunks = tm // GLA_CHUNK

    @pl.when(t == 0)
    def _():
        hist_ref[...] = jnp.zeros(hist_ref.shape, F32)
        st_ref[...] = jnp.zeros(st_ref.shape, F32)

    rest = rest_ref[...]

    xp = rest[:, R_POOL:R_POOL + POOL_WIDTH]
    row = _iota((tm, POOL_WIDTH), 0)

    def shifted(cur, slot, k):
        prev = hist_ref[slot]
        hist_ref[slot] = cur
        return jnp.where(row >= k, pltpu.roll(cur, k, axis=0), pltpu.roll(prev, k, axis=0))

    s2 = xp + shifted(xp, 0, 1)
    s4 = s2 + shifted(s2, 1, 2)
    s8 = s4 + shifted(s4, 2, 4)
    s16 = s8 + shifted(s8, 3, 8)
    window = 2 << (_iota((1, POOL_WIDTH), 1) >> 6)
    count = jnp.minimum(t * tm + row + 1, window).astype(F32)
    pooled = _pool_select(s2, s4, s8, s16, count, xp)
    o_pool = _dot(pooled.astype(BF16), wpool_ref[...]) * pscale_ref[...]

    q = rest[:, R_Q:R_Q + 128]
    k = rest[:, R_K:R_K + 128]
    v = rest[:, R_V:R_V + GLA_WIDTH]
    la = rest[:, R_LOGA:R_LOGA + 128]
    r_i = _iota((CUMSUM_ROWS, CUMSUM_ROWS), 0)
    c_i = _iota((CUMSUM_ROWS, CUMSUM_ROWS), 1)
    same_chunk = (r_i >> 6) == (c_i >> 6)
    tri_bd = (same_chunk & (c_i <= r_i)).astype(BF16)
    ones_bd = same_chunk.astype(BF16)
    b, b_last = [], []
    for r0 in range(0, tm, CUMSUM_ROWS):
        la_parts = _split3(la[r0:r0 + CUMSUM_ROWS])
        b.append(_dot3(tri_bd, la_parts))
        b_last.append(_dot3(ones_bd, la_parts))
    b = jnp.concatenate(b, axis=0)
    b_last = jnp.concatenate(b_last, axis=0)
    qt = q * GLA_SCALE * jnp.exp(b)
    kt = k * jnp.exp(-b)
    kd = k * jnp.exp(b_last - b)
    dec = jnp.exp(b_last)
    head_mask = ((_iota((4 * GLA_CHUNK, 128), 0) >> 6) == (_iota((4 * GLA_CHUNK, 128), 1) >> 5)).astype(F32)
    causal = (_iota((4 * GLA_CHUNK, GLA_CHUNK), 0) & (GLA_CHUNK - 1)) >= _iota((4 * GLA_CHUNK, GLA_CHUNK), 1)
    lane_head = _iota((GLA_CHUNK, GLA_WIDTH), 1) >> 6
    for n in range(n_chunks):
        lo, hi = n * GLA_CHUNK, (n + 1) * GLA_CHUNK
        q_n = qt[lo:hi]
        v_n = v[lo:hi].astype(BF16)
        q_bd = (jnp.concatenate([q_n] * GLA_HEADS, axis=0) * head_mask).astype(BF16)
        a = _dot_nt(q_bd, kt[lo:hi].astype(BF16))
        a = jnp.where(causal, a, 0.0)
        pv = _dot(a.astype(BF16), v_n)
        o_n = jnp.zeros((GLA_CHUNK, GLA_WIDTH), F32)
        for hh in range(GLA_HEADS):
            o_n = o_n + jnp.where(lane_head == hh, pv[hh * GLA_CHUNK:(hh + 1) * GLA_CHUNK], 0.0)
        st = st_ref[...]
        o_n = o_n + _dot_nt(q_n.astype(BF16), st.astype(BF16))
        upd = _dot_tn(v_n, kd[lo:hi].astype(BF16))
        st_ref[...] = dec[lo:lo + 1, :] * st + upd * head_mask
        o_ref[lo:hi, :] = o_n
    o_gla = _head_rms(o_ref[...], ggla_ref[...])

    hout_ref[...] = _gate_and_project(omla_ref[...], rest, o_pool, o_gla, h_ref[...], wout_ref,
                                      gpost_ref[...])

    @pl.when(t == pl.num_programs(1) - 1)
    def _():
        sout_ref[...] = st_ref[...]


def _mix_prompt(rest, omla, h, w, *, batch, seq, tm):
    n = batch * seq
    nt = seq // tm
    full = lambda a: pl.BlockSpec(a.shape, lambda b, t, nd=a.ndim: (0,) * nd)
    row = lambda width: pl.BlockSpec((tm, width), lambda b, t: (b * nt + t, 0))
    weights = (w["w_pool"], w["pool_scale"], w["g_gla"], w["w_out"], w["g_post"])
    return pl.pallas_call(
        functools.partial(_mix_prompt_body, tm=tm),
        grid=(batch, nt),
        in_specs=[row(D_REST), row(512), row(D_MODEL)] + [full(a) for a in weights],
        out_specs=[row(D_MODEL), pl.BlockSpec((None, 4 * GLA_DV, 128), lambda b, t: (b, 0, 0))],
        out_shape=[jax.ShapeDtypeStruct((n, D_MODEL), F32),
                   jax.ShapeDtypeStruct((batch, 4 * GLA_DV, 128), F32)],
        scratch_shapes=[pltpu.VMEM((4, tm, POOL_WIDTH), F32), pltpu.VMEM((4 * GLA_DV, 128), F32),
                        pltpu.VMEM((tm, GLA_WIDTH), F32)],
        compiler_params=pltpu.CompilerParams(dimension_semantics=("parallel", "arbitrary"),
                                             vmem_limit_bytes=VMEM_LIMIT_BYTES),
        name="mix_prompt",
    )(rest, omla, h, *weights)


def _mix_sample_body(rest_ref, olat_ref, prefix_ref, s0_ref, h_ref, wuv_ref, wpool_ref, pscale_ref,
                     ggla_ref, wout_ref, gpost_ref,
                     hout_ref, snew_ref, st_ref, qT_ref, kT_ref, aT_ref, vT_ref, oT_ref, o_ref,
                     *, t_new, past_len):
    bs = LANES
    n_state_blocks = (GLA_HEADS * GLA_DK * GLA_DV) // LANES
    rest = rest_ref[...]

    o_mla = jnp.concatenate([_dot(olat_ref[h].astype(BF16), wuv_ref[h]) for h in range(MLA_HEADS)], axis=1)

    z = [prefix_ref[j] for j in range(POOL_BUF)]
    z += [rest[t * bs:(t + 1) * bs, R_POOL:R_POOL + POOL_WIDTH] for t in range(t_new)]
    pooled = []
    for t in range(t_new):
        e = POOL_BUF + t
        s2 = z[e] + z[e - 1]
        s4 = s2 + z[e - 2] + z[e - 3]
        s8 = s4 + z[e - 4] + z[e - 5] + z[e - 6] + z[e - 7]
        s16 = s8
        for i in range(8, 16):
            s16 = s16 + z[e - i]
        window = 2 << (_iota((1, POOL_WIDTH), 1) >> 6)
        count = jnp.minimum(past_len + t + 1, window).astype(F32)
        pooled.append(_pool_select(s2, s4, s8, s16, count, z[e]))
    pooled = jnp.concatenate(pooled, axis=0)
    o_pool = _dot(pooled.astype(BF16), wpool_ref[...]) * pscale_ref[...]

    for j in range(n_state_blocks):
        st_ref[j * LANES:(j + 1) * LANES, :] = s0_ref[:, j * LANES:(j + 1) * LANES].T
    for t in range(t_new):
        rows = slice(t * bs, (t + 1) * bs)
        qT_ref[t] = (rest[rows, R_Q:R_Q + 128] * GLA_SCALE).T
        kT_ref[t] = rest[rows, R_K:R_K + 128].T
        aT_ref[t] = jnp.exp(rest[rows, R_LOGA:R_LOGA + 128]).T
        vT_ref[t, 0:LANES] = rest[rows, R_V:R_V + LANES].T
        vT_ref[t, LANES:2 * LANES] = rest[rows, R_V + LANES:R_V + 2 * LANES].T
    for hh in range(GLA_HEADS):
        v_blocks = [vT_ref[t, hh * GLA_DV:(hh + 1) * GLA_DV, :] for t in range(t_new)]

        def body(d, carry, hh=hh, v_blocks=v_blocks):
            r = hh * GLA_DK + d
            base = pl.multiple_of(r * GLA_DV, GLA_DV)
            s = st_ref[pl.ds(base, GLA_DV), :]
            outs = []
            for t in range(t_new):
                s = aT_ref[t, pl.ds(r, 1), :] * s + kT_ref[t, pl.ds(r, 1), :] * v_blocks[t]
                outs.append(carry[t] + qT_ref[t, pl.ds(r, 1), :] * s)
            st_ref[pl.ds(base, GLA_DV), :] = s
            return tuple(outs)

        acc = lax.fori_loop(0, GLA_DK, body, tuple(jnp.zeros((GLA_DV, LANES), F32) for _ in range(t_new)))
        for t in range(t_new):
            oT_ref[t, hh * GLA_DV:(hh + 1) * GLA_DV, :] = acc[t]
    for j in range(n_state_blocks):
        snew_ref[:, j * LANES:(j + 1) * LANES] = st_ref[j * LANES:(j + 1) * LANES, :].T
    for t in range(t_new):
        o_ref[t * bs:(t + 1) * bs, 0:LANES] = oT_ref[t, 0:LANES, :].T
        o_ref[t * bs:(t + 1) * bs, LANES:2 * LANES] = oT_ref[t, LANES:2 * LANES, :].T
    o_gla = _head_rms(o_ref[...], ggla_ref[...])

    hout_ref[...] = _gate_and_project(o_mla, rest, o_pool, o_gla, h_ref[...], wout_ref, gpost_ref[...])


def _mix_sample(rest, olat, prefix_t, s0, h, w, *, t_new, past_len):
    n = rest.shape[0]
    bs = n // t_new
    assert bs == LANES, "sample mixing kernel keeps the sequences on the lane axis"
    state_w = GLA_HEADS * GLA_DK * GLA_DV
    args = (rest, olat, prefix_t, s0, h, w["w_uv"], w["w_pool"], w["pool_scale"], w["g_gla"], w["w_out"],
            w["g_post"])
    full = lambda a: pl.BlockSpec(a.shape, lambda i, nd=a.ndim: (0,) * nd)
    return pl.pallas_call(
        functools.partial(_mix_sample_body, t_new=t_new, past_len=past_len),
        grid=(1,),
        in_specs=[full(a) for a in args],
        out_specs=[pl.BlockSpec((n, D_MODEL), lambda i: (0, 0)), pl.BlockSpec((bs, state_w), lambda i: (0, 0))],
        out_shape=[jax.ShapeDtypeStruct((n, D_MODEL), F32), jax.ShapeDtypeStruct((bs, state_w), F32)],
        scratch_shapes=[
            pltpu.VMEM((state_w, LANES), F32),
            pltpu.VMEM((t_new, 128, LANES), F32), pltpu.VMEM((t_new, 128, LANES), F32),
            pltpu.VMEM((t_new, 128, LANES), F32), pltpu.VMEM((t_new, GLA_WIDTH, LANES), F32),
            pltpu.VMEM((t_new, GLA_WIDTH, LANES), F32), pltpu.VMEM((n, GLA_WIDTH), F32),
        ],
        compiler_params=pltpu.CompilerParams(dimension_semantics=("arbitrary",),
                                             vmem_limit_bytes=VMEM_LIMIT_BYTES),
        name="mix_sample",
    )(*args)


def _rope_table(pos):
    half = MLA_ROPE // 2
    inv = 1.0 / (ROPE_BASE ** (jnp.arange(half, dtype=F32) / half))
    ang = pos.astype(F32)[:, None] * inv[None, :]
    cos, sin = jnp.cos(ang), jnp.sin(ang)
    return jnp.concatenate([cos, cos, -sin, sin], axis=1)


def _layer_weights(l, g_pre, w_in, g_cq, w_uq, g_ckv, w_uk, w_uv, w_pool, pool_scale, w_a2, b_a, g_gla,
                   w_out, g_post):
    half = MLA_ROPE // 2
    wi = w_in[l]
    w_in_ext = jnp.concatenate([
        wi[:, 0:576],
        wi[:, 544:576], wi[:, 512:544],
        wi[:, 576:2112],
        wi[:, 2128:2384],
        wi[:, 2112:2128],
        jnp.zeros((D_MODEL, LANES - GLA_GATE_RANK), F32),
    ], axis=1).astype(BF16)
    wq = w_uq[l]
    rope_cols = wq[:, :, MLA_NOPE:]
    w_uq_ext = jnp.concatenate([wq, rope_cols[:, :, half:], rope_cols[:, :, :half]], axis=2)
    w_uq_ext = w_uq_ext.reshape(Q_RANK, MLA_HEADS * 256).astype(BF16)
    w_pool_bd = jnp.zeros((POOL_WIDTH, POOL_WIDTH), F32)
    for g in range(len(POOL_WINDOWS)):
        w_pool_bd = w_pool_bd.at[g * POOL_GC:(g + 1) * POOL_GC, g * POOL_GC:(g + 1) * POOL_GC].set(w_pool[l, g])
    w_a2_pad = jnp.zeros((LANES, GLA_HEADS * GLA_DK), F32).at[:GLA_GATE_RANK].set(w_a2[l])
    return {
        "g_pre": g_pre[l][None, :], "w_in": w_in_ext, "g_cq": g_cq[l][None, :], "w_uq": w_uq_ext,
        "g_ckv": g_ckv[l][None, :],
        "w_uk": jnp.transpose(w_uk[l], (1, 2, 0)).astype(BF16),
        "w_uv": jnp.transpose(w_uv[l], (1, 0, 2)).astype(BF16),
        "w_pool": w_pool_bd.astype(BF16), "pool_scale": pool_scale[l][None, :],
        "w_a2": w_a2_pad.astype(BF16), "b_a": b_a[l][None, :], "g_gla": g_gla[l][None, :],
        "w_out": w_out[l].astype(BF16), "g_post": g_post[l][None, :],
    }


def _tile_sizes(seq):
    pick = lambda want: max(t for t in (256, 512) if t <= want and seq % t == 0)
    return pick(512), pick(512), pick(512)


@jax.jit
def _forward(x_prompt, x_sample, cache_ckv, cache_krope, state_pool, state_gla, page_table,
             g_pre, w_in, g_cq, w_uq, g_ckv, w_uk, w_uv, w_pool, pool_scale, w_a2, b_a, g_gla, w_out, g_post):
    bp, tp, _ = x_prompt.shape
    bs, ts, _ = x_sample.shape
    depth = w_in.shape[0]
    n_pages = page_table.shape[1]
    past_len = n_pages * PAGE_SIZE
    tm_in, tq, tm_mix = _tile_sizes(tp)
    assert n_pages % (2 * ROPE_SUB) == 0 and MLA_HEADS * ts == 16
    pages_per_step = max(p for p in (8, 16, 32) if n_pages % (2 * p) == 0)
    cache_krope_t = jnp.swapaxes(cache_krope, 2, 3)

    tab_p = _rope_table(jnp.arange(tp))
    tab_s = jnp.repeat(_rope_table(past_len + jnp.arange(ts)), bs, axis=0)
    pt_flat = page_table.reshape(-1).astype(jnp.int32)

    hp = x_prompt.reshape(bp * tp, D_MODEL)
    hs = jnp.transpose(x_sample, (1, 0, 2)).reshape(ts * bs, D_MODEL)
    outs = {k: [] for k in ("ckv_p", "kr_p", "pool_p", "gla_p", "ckv_s", "kr_s", "pool_s", "gla_s")}
    for l in range(depth):
        w = _layer_weights(l, g_pre, w_in, g_cq, w_uq, g_ckv, w_uk, w_uv, w_pool, pool_scale, w_a2, b_a,
                           g_gla, w_out, g_post)
        qc, ckv, kr, kc, rest = _inproj(hp, tab_p, w, tm=tm_in)
        omla = _attn_prompt(qc, kc, w["w_uv"], batch=bp, seq=tp, tq=tq)
        hp, st = _mix_prompt(rest, omla, hp, w, batch=bp, seq=tp, tm=tm_mix)
        outs["ckv_p"].append(ckv.reshape(bp, tp, KV_RANK))
        outs["kr_p"].append(kr.reshape(bp, tp, MLA_ROPE))
        outs["pool_p"].append(rest.reshape(bp, tp, D_REST)[:, tp - POOL_BUF:, R_POOL:R_POOL + POOL_WIDTH])
        st = st.reshape(bp, GLA_HEADS, GLA_DV, GLA_HEADS, GLA_DK)
        st = jnp.stack([st[:, h, :, h, :] for h in range(GLA_HEADS)], axis=1)
        outs["gla_p"].append(jnp.swapaxes(st, 2, 3))
        qc, ckv, kr, _, rest = _inproj(hs, tab_s, w, tm=ts * bs)
        q_b = jnp.transpose(qc.reshape(MLA_HEADS, ts, bs, D_QK), (2, 0, 1, 3)).reshape(bs, MLA_HEADS * ts, D_QK)
        ckv_b = jnp.transpose(ckv.reshape(ts, bs, KV_RANK), (1, 0, 2))
        kr_b = jnp.transpose(kr.reshape(ts, bs, MLA_ROPE), (1, 0, 2))
        olat = _attn_sample(pt_flat, q_b, ckv_b, kr_b, cache_ckv, cache_krope_t, layer=l, n_pages=n_pages,
                            pages=pages_per_step)
        olat = jnp.transpose(olat.reshape(bs, MLA_HEADS, ts, KV_RANK), (1, 2, 0, 3)).reshape(
            MLA_HEADS, ts * bs, KV_RANK)
        prefix_t = jnp.transpose(state_pool[l], (1, 0, 2))
        s0 = state_gla[l].reshape(bs, GLA_HEADS * GLA_DK * GLA_DV)
        hs, s_new = _mix_sample(rest, olat, prefix_t, s0, hs, w, t_new=ts, past_len=past_len)
        xp_b = jnp.transpose(rest[:, R_POOL:R_POOL + POOL_WIDTH].reshape(ts, bs, POOL_WIDTH), (1, 0, 2))
        outs["ckv_s"].append(ckv_b)
        outs["kr_s"].append(kr_b)
        outs["pool_s"].append(jnp.concatenate([state_pool[l], xp_b], axis=1)[:, ts:])
        outs["gla_s"].append(s_new.reshape(bs, GLA_HEADS, GLA_DK, GLA_DV))
    y_prompt = hp.reshape(bp, tp, D_MODEL)
    y_sample = jnp.transpose(hs.reshape(ts, bs, D_MODEL), (1, 0, 2))
    return (y_prompt, y_sample,
            jnp.stack(outs["ckv_p"]), jnp.stack(outs["kr_p"]), jnp.stack(outs["pool_p"]), jnp.stack(outs["gla_p"]),
            jnp.stack(outs["ckv_s"]), jnp.stack(outs["kr_s"]), jnp.stack(outs["pool_s"]), jnp.stack(outs["gla_s"]))


def kernel(x_prompt, x_sample, cache_ckv, cache_krope, state_pool, state_gla, page_table, g_pre, w_in, g_cq,
           w_uq, g_ckv, w_uk, w_uv, w_pool, pool_scale, w_a2, b_a, g_gla, w_out, g_post):
    return _forward(x_prompt, x_sample, cache_ckv, cache_krope, state_pool, state_gla, page_table, g_pre, w_in,
                    g_cq, w_uq, g_ckv, w_uk, w_uv, w_pool, pool_scale, w_a2, b_a, g_gla, w_out, g_post)
```

```python
import functools

import jax
import jax.numpy as jnp
from jax import lax
from jax.experimental import pallas as pl
from jax.experimental.pallas import tpu as pltpu

F32 = jnp.float32
BF16 = jnp.bfloat16

D_MODEL = 1024
MLA_HEADS = 4
MLA_NOPE = 128
MLA_ROPE = 64
MLA_V = 128
Q_RANK = 256
KV_RANK = 256
ROPE_BASE = 10000.0
POOL_WINDOWS = (2, 4, 8, 16)
POOL_WIDTH = 256
POOL_GC = 64
POOL_BUF = 15
GLA_HEADS = 4
GLA_DK = 32
GLA_DV = 64
GLA_WIDTH = 256
GLA_GATE_RANK = 16
GLA_TAU = 16.0
GLA_CHUNK = 64
CUMSUM_ROWS = 4 * GLA_CHUNK
NORM_EPS = 1e-6
PAGE_SIZE = 128
QK_SCALE = (MLA_NOPE + MLA_ROPE) ** -0.5 * 1.4426950408889634
GLA_SCALE = GLA_DK ** -0.5

LANES = 128
VMEM_LIMIT_BYTES = 56 * 1024 * 1024

C_CQ = 0
C_CKV = 256
C_KR = 512
C_REST = 640
C_A = 2432
D_EXT = 2560
R_GMLA = 0
R_POOL = 512
R_GPOOL = 768
R_Q = 1024
R_K = 1152
R_V = 1280
R_GGLA = 1536
R_LOGA = 1792
D_REST = 1920
D_QK = KV_RANK + LANES

NEG_BIG = -1e30


def _dot(a, b):
    return jnp.dot(a, b, preferred_element_type=F32)


def _dot_nt(a, b):
    return lax.dot_general(a, b, (((1,), (1,)), ((), ())), preferred_element_type=F32)


def _dot_tn(a, b):
    return lax.dot_general(a, b, (((0,), (0,)), ((), ())), preferred_element_type=F32)


def _rms(x, g):
    return x * lax.rsqrt(jnp.mean(x * x, axis=-1, keepdims=True) + NORM_EPS) * g


def _silu(x):
    half = 0.5 * x
    return half + half * jnp.tanh(half)


def _split3(x):
    hi = x.astype(BF16)
    r = x - hi.astype(F32)
    mid = r.astype(BF16)
    lo = (r - mid.astype(F32)).astype(BF16)
    return hi, mid, lo


def _dot3(mat, parts):
    return _dot(mat, parts[0]) + _dot(mat, parts[1]) + _dot(mat, parts[2])


def _iota(shape, dim):
    return lax.broadcasted_iota(jnp.int32, shape, dim)


def _inproj_body(x_ref, tab_ref, gpre_ref, win_ref, gcq_ref, wuq_ref, gckv_ref, wuk_ref,
                 wa2_ref, ba_ref,
                 qc_ref, ckv_ref, kr_ref, kc_ref, rest_ref):
    u = _rms(x_ref[...], gpre_ref[...])
    p = _dot(u.astype(BF16), win_ref[...])
    tab = tab_ref[...]
    low_half = (_iota((1, LANES), 1) < MLA_ROPE).astype(F32)

    def rope(slab):
        t = slab * tab
        return t + pltpu.roll(t, MLA_ROPE, axis=1)

    cq = _rms(p[:, C_CQ:C_CQ + Q_RANK], gcq_ref[...])
    q = _dot(cq.astype(BF16), wuq_ref[...])
    for h in range(MLA_HEADS):
        base = h * 256
        q_nope = q[:, base:base + MLA_NOPE]
        q_lat = _dot(q_nope.astype(BF16), wuk_ref[h]) * QK_SCALE
        q_rope = rope(q[:, base + MLA_NOPE:base + 256]) * (low_half * QK_SCALE)
        qc_ref[h] = jnp.concatenate([q_lat, q_rope], axis=1).astype(BF16)

    ckv = _rms(p[:, C_CKV:C_CKV + KV_RANK], gckv_ref[...])
    ckv_ref[...] = ckv
    kr = rope(p[:, C_KR:C_KR + LANES])
    kr_ref[...] = kr[:, :MLA_ROPE]
    kc_ref[...] = jnp.concatenate([ckv, kr * low_half], axis=1).astype(BF16)

    rest_ref[:, 0:R_LOGA] = p[:, C_REST:C_A]
    z = _dot(p[:, C_A:D_EXT].astype(BF16), wa2_ref[...]) + ba_ref[...]
    log_sig = jnp.minimum(z, 0.0) - jnp.log(1.0 + jnp.exp(-jnp.abs(z)))
    rest_ref[:, R_LOGA:D_REST] = log_sig * (1.0 / GLA_TAU)


def _inproj(x, tab, w, *, tm):
    n = x.shape[0]
    nt = tab.shape[0] // tm
    full = lambda a: pl.BlockSpec(a.shape, lambda i, nd=a.ndim: (0,) * nd)
    row = lambda width: pl.BlockSpec((tm, width), lambda i: (i, 0))
    head = lambda width: pl.BlockSpec((MLA_HEADS, tm, width), lambda i: (0, i, 0))
    weights = (w["g_pre"], w["w_in"], w["g_cq"], w["w_uq"], w["g_ckv"], w["w_uk"], w["w_a2"], w["b_a"])
    return pl.pallas_call(
        _inproj_body,
        grid=(n // tm,),
        in_specs=[row(D_MODEL), pl.BlockSpec((tm, LANES), lambda i: (i % nt, 0))]
        + [full(a) for a in weights],
        out_specs=[head(D_QK), row(KV_RANK), row(MLA_ROPE), row(D_QK), row(D_REST)],
        out_shape=[
            jax.ShapeDtypeStruct((MLA_HEADS, n, D_QK), BF16),
            jax.ShapeDtypeStruct((n, KV_RANK), F32),
            jax.ShapeDtypeStruct((n, MLA_ROPE), F32),
            jax.ShapeDtypeStruct((n, D_QK), BF16),
            jax.ShapeDtypeStruct((n, D_REST), F32),
        ],
        compiler_params=pltpu.CompilerParams(dimension_semantics=("parallel",),
                                             vmem_limit_bytes=VMEM_LIMIT_BYTES),
        name="inproj",
    )(x, tab, *weights)


def _attn_prompt_body(q_ref, k_ref, wuv_ref, o_ref, s_ref, m_ref, l_ref, acc_ref, *, tq):
    i = pl.program_id(1)

    def scores(j, h):
        start = pl.multiple_of(j * tq, tq)
        return _dot_nt(q_ref[h], k_ref[pl.ds(start, tq), :])

    def update(rs, s, v, first):
        m_cur = jnp.max(s, axis=1, keepdims=True)
        m_new = jnp.broadcast_to(m_cur, (s.shape[0], LANES)) if first else jnp.maximum(m_ref[rs], m_cur)
        p = [jnp.exp2(s[:, c * LANES:(c + 1) * LANES] - m_new) for c in range(s.shape[1] // LANES)]
        l_new = p[0]
        for pc in p[1:]:
            l_new = l_new + pc
        pv = _dot(jnp.concatenate(p, axis=1).astype(BF16), v)
        if first:
            l_ref[rs] = l_new
            acc_ref[rs] = pv
        else:
            alpha = jnp.exp2(m_ref[rs] - m_new)
            l_ref[rs] = alpha * l_ref[rs] + l_new
            acc_ref[rs] = jnp.concatenate([alpha] * (KV_RANK // LANES), axis=1) * acc_ref[rs] + pv
        m_ref[rs] = m_new

    def consume(j, h, s, diagonal, first):
        start = pl.multiple_of(j * tq, tq)
        if not diagonal:
            update(slice(h * tq, (h + 1) * tq), s, k_ref[pl.ds(start, tq), 0:KV_RANK], first)
            return
        half = tq // 2
        for r0, n_keys in ((0, half), (half, tq)):
            sp = s[r0:r0 + half, 0:n_keys]
            visible = _iota((half, n_keys), 1) <= _iota((half, n_keys), 0) + r0
            update(slice(h * tq + r0, h * tq + r0 + half), jnp.where(visible, sp, NEG_BIG),
                   k_ref[pl.ds(start, n_keys), 0:KV_RANK], first)

    for h in range(MLA_HEADS):
        s_ref[0, h * tq:(h + 1) * tq] = scores(0, h)

    def step(j, slot, last, first=False):
        for h in range(MLA_HEADS):
            if not last:
                s_ref[1 - slot, h * tq:(h + 1) * tq] = scores(j + 1, h)
            consume(j, h, s_ref[slot, h * tq:(h + 1) * tq], last, first)

    def body(j, carry):
        for slot in range(2):
            @pl.when((j & 1) == slot)
            def _():
                step(j, slot, False)
        return carry

    @pl.when(i == 0)
    def _():
        step(0, 0, True, first=True)

    @pl.when(i > 0)
    def _():
        step(0, 0, False, first=True)
        lax.fori_loop(1, i, body, 0)
        for slot in range(2):
            @pl.when((i & 1) == slot)
            def _():
                step(i, slot, True)

    l = jnp.sum(l_ref[...], axis=1, keepdims=True)
    o = acc_ref[...] * (1.0 / l)
    for h in range(MLA_HEADS):
        o_h = o[h * tq:(h + 1) * tq].astype(BF16)
        o_ref[:, h * MLA_V:(h + 1) * MLA_V] = _dot(o_h, wuv_ref[h])


def _attn_prompt(qc, kc, wuv, *, batch, seq, tq):
    n = batch * seq
    nq = seq // tq
    rows = MLA_HEADS * tq
    return pl.pallas_call(
        functools.partial(_attn_prompt_body, tq=tq),
        grid=(batch, nq),
        in_specs=[
            pl.BlockSpec((MLA_HEADS, tq, D_QK), lambda b, i: (0, b * nq + i, 0)),
            pl.BlockSpec((seq, D_QK), lambda b, i: (b, 0)),
            pl.BlockSpec(wuv.shape, lambda b, i: (0, 0, 0)),
        ],
        out_specs=pl.BlockSpec((tq, MLA_HEADS * MLA_V), lambda b, i: (b * nq + i, 0)),
        out_shape=jax.ShapeDtypeStruct((n, MLA_HEADS * MLA_V), F32),
        scratch_shapes=[pltpu.VMEM((2, rows, tq), F32), pltpu.VMEM((rows, LANES), F32),
                        pltpu.VMEM((rows, LANES), F32), pltpu.VMEM((rows, KV_RANK), F32)],
        compiler_params=pltpu.CompilerParams(dimension_semantics=("parallel", "arbitrary"),
                                             vmem_limit_bytes=VMEM_LIMIT_BYTES),
        name="attn_prompt",
    )(qc, kc, wuv)


ROPE_STACK = 2 * LANES // MLA_ROPE
ROPE_SUB = 2 * ROPE_STACK
N_SLOTS = 4


def _attn_sample_body(pt_ref, q_ref, cn_ref, krn_ref, ckv_hbm, kr_hbm, o_ref, kv_buf, kr_buf, sem,
                      *, layer, n_pages, pages, t_new):
    b = pl.program_id(0)
    n_chunks = n_pages // pages
    rows = MLA_HEADS * t_new

    def page_copies(page, i, slot):
        k, p_i = i // ROPE_SUB, i % ROPE_SUB
        g, u = p_i % ROPE_STACK, p_i // ROPE_STACK
        pos = k * ROPE_SUB + g * 2 + u
        return (
            pltpu.make_async_copy(ckv_hbm.at[layer, page],
                                  kv_buf.at[slot, pl.ds(pos * PAGE_SIZE, PAGE_SIZE), :], sem.at[0, slot]),
            pltpu.make_async_copy(kr_hbm.at[layer, page],
                                  kr_buf.at[slot, 2 * k + u, pl.ds(g * MLA_ROPE, MLA_ROPE), :], sem.at[1, slot]),
        )

    def start_chunk(seq, c, slot):
        for i in range(pages):
            for cp in page_copies(pt_ref[seq * n_pages + c * pages + i], i, slot):
                cp.start()

    def wait_chunk(slot):
        for i in range(pages):
            for cp in page_copies(0, i, slot):
                cp.wait()

    ql = q_ref[0, :, 0:KV_RANK]
    qlf = ql.astype(F32)
    qr = q_ref[0, :, KV_RANK:D_QK].astype(F32)
    qr2 = qr + pltpu.roll(qr, MLA_ROPE, axis=1)
    q_wide = jnp.concatenate([qr2, qr2], axis=1)
    bd_shape = (ROPE_STACK * rows, ROPE_STACK * MLA_ROPE)
    on_block = (_iota(bd_shape, 0) >> 4) == (_iota(bd_shape, 1) >> 6)
    q_bd = jnp.where(on_block, jnp.concatenate([q_wide] * ROPE_STACK, axis=0), 0.0).astype(BF16)

    qrf = qr[:, 0:MLA_ROPE]
    cn = cn_ref[0]
    krn = krn_ref[0]
    t_row = _iota((rows, 1), 0) & (t_new - 1)
    scores = []
    for t2 in range(t_new):
        sc = (jnp.sum(qlf * cn[t2:t2 + 1, :], axis=1, keepdims=True)
              + jnp.sum(qrf * krn[t2:t2 + 1, :], axis=1, keepdims=True))
        scores.append(jnp.where(t2 <= t_row, sc, NEG_BIG))
    m = scores[0]
    for sc in scores[1:]:
        m = jnp.maximum(m, sc)
    l = jnp.zeros((rows, 1), F32)
    acc = jnp.zeros((rows, KV_RANK), F32)
    for t2 in range(t_new):
        p = jnp.exp2(scores[t2] - m)
        l = l + p
        acc = acc + p * cn[t2:t2 + 1, :]

    first = b * n_chunks

    def slot_of(c):
        if n_chunks % N_SLOTS == 0:
            return c % N_SLOTS
        return lax.rem(first + c, N_SLOTS)

    def chunk_scores(c):
        slot = slot_of(c)
        wait_chunk(slot)
        kbs, s_parts = [], []
        for k in range(pages // ROPE_SUB):
            kb = kv_buf[slot, k * ROPE_SUB * PAGE_SIZE:(k + 1) * ROPE_SUB * PAGE_SIZE, :].astype(BF16)
            rb = jnp.concatenate([kr_buf[slot, 2 * k], kr_buf[slot, 2 * k + 1]], axis=1).astype(BF16)
            s_rope = _dot(q_bd, rb)
            s_rope = jnp.concatenate([s_rope[g * rows:(g + 1) * rows] for g in range(ROPE_STACK)], axis=1)
            s_parts.append(_dot_nt(ql, kb) + s_rope)
            kbs.append(kb)
        return jnp.concatenate(s_parts, axis=1), kbs

    @pl.when(b == 0)
    def _():
        for c0 in range(N_SLOTS - 1):
            start_chunk(c0 // n_chunks, c0 % n_chunks, c0)

    nxt = chunk_scores(0)
    for c in range(n_chunks):
        ahead = c + N_SLOTS - 1
        if ahead < n_chunks:
            start_chunk(b, ahead, slot_of(ahead))
        else:
            @pl.when(b + ahead // n_chunks < pl.num_programs(0))
            def _():
                start_chunk(b + ahead // n_chunks, ahead % n_chunks, slot_of(ahead))
        s, kbs = nxt
        if c + 1 < n_chunks:
            nxt = chunk_scores(c + 1)
        m_new = jnp.maximum(m, jnp.max(s, axis=1, keepdims=True))
        alpha = jnp.exp2(m - m_new)
        p = jnp.exp2(s - m_new)
        l = alpha * l + jnp.sum(p, axis=1, keepdims=True)
        acc = alpha * acc
        for k, kb in enumerate(kbs):
            acc = acc + _dot(p[:, k * ROPE_SUB * PAGE_SIZE:(k + 1) * ROPE_SUB * PAGE_SIZE].astype(BF16), kb)
        m = m_new

    o_ref[0] = acc * (1.0 / l)


def _attn_sample(page_table_flat, q, ckv_new, kr_new, cache_ckv, cache_krope_t, *, layer, n_pages, pages):
    bs, rows, _ = q.shape
    t_new = ckv_new.shape[1]
    assert pages % ROPE_SUB == 0 and bs * (n_pages // pages) >= N_SLOTS
    grid_spec = pltpu.PrefetchScalarGridSpec(
        num_scalar_prefetch=1,
        grid=(bs,),
        in_specs=[
            pl.BlockSpec((1, rows, D_QK), lambda b, pt: (b, 0, 0)),
            pl.BlockSpec((1, t_new, KV_RANK), lambda b, pt: (b, 0, 0)),
            pl.BlockSpec((1, t_new, MLA_ROPE), lambda b, pt: (b, 0, 0)),
            pl.BlockSpec(memory_space=pl.ANY),
            pl.BlockSpec(memory_space=pl.ANY),
        ],
        out_specs=pl.BlockSpec((1, rows, KV_RANK), lambda b, pt: (b, 0, 0)),
        scratch_shapes=[
            pltpu.VMEM((N_SLOTS, pages * PAGE_SIZE, KV_RANK), F32),
            pltpu.VMEM((N_SLOTS, pages // ROPE_STACK, ROPE_STACK * MLA_ROPE, PAGE_SIZE), F32),
            pltpu.SemaphoreType.DMA((2, N_SLOTS)),
        ],
    )
    return pl.pallas_call(
        functools.partial(_attn_sample_body, layer=layer, n_pages=n_pages, pages=pages, t_new=t_new),
        grid_spec=grid_spec,
        out_shape=jax.ShapeDtypeStruct((bs, rows, KV_RANK), F32),
        compiler_params=pltpu.CompilerParams(dimension_semantics=("arbitrary",),
                                             vmem_limit_bytes=VMEM_LIMIT_BYTES),
        name="attn_sample",
    )(page_table_flat, q, ckv_new, kr_new, cache_ckv, cache_krope_t)


def _head_rms(o, g):
    o2 = o * o
    hi = o2.astype(BF16)
    lo = (o2 - hi.astype(F32)).astype(BF16)
    same_head = (_iota((GLA_WIDTH, GLA_WIDTH), 0) >> 6) == (_iota((GLA_WIDTH, GLA_WIDTH), 1) >> 6)
    ones_bd = same_head.astype(BF16)
    ms = (_dot(hi, ones_bd) + _dot(lo, ones_bd)) * (1.0 / GLA_DV)
    return o * lax.rsqrt(ms + NORM_EPS) * g


def _pool_select(s2, s4, s8, s16, count, xp):
    group = _iota((1, POOL_WIDTH), 1) >> 6
    win = jnp.where(group == 0, s2, jnp.where(group == 1, s4, jnp.where(group == 2, s8, s16)))
    return win / count - xp


def _gate_and_project(o_mla, rest, o_pool, o_gla, h, wout_ref, gpost):
    mixed = jnp.concatenate([
        o_mla * _silu(rest[:, R_GMLA:R_GMLA + 512]),
        o_pool * _silu(rest[:, R_GPOOL:R_GPOOL + POOL_WIDTH]),
        o_gla * _silu(rest[:, R_GGLA:R_GGLA + GLA_WIDTH]),
    ], axis=1).astype(BF16)
    y = _dot(mixed, wout_ref[...])
    return h + _rms(y, gpost)


def _mix_prompt_body(rest_ref, omla_ref, h_ref, wpool_ref, pscale_ref, ggla_ref, wout_ref, gpost_ref,
                     hout_ref, sout_ref, hist_ref, st_ref, o_ref, *, tm):
    t = pl.program_id(1)
    n_chunks = tm // GLA_CHUNK

    @pl.when(t == 0)
    def _():
        hist_ref[...] = jnp.zeros(hist_ref.shape, F32)
        st_ref[...] = jnp.zeros(st_ref.shape, F32)

    rest = rest_ref[...]

    xp = rest[:, R_POOL:R_POOL + POOL_WIDTH]
    row = _iota((tm, POOL_WIDTH), 0)

    def shifted(cur, slot, k):
        prev = hist_ref[slot]
        hist_ref[slot] = cur
        return jnp.where(row >= k, pltpu.roll(cur, k, axis=0), pltpu.roll(prev, k, axis=0))

    s2 = xp + shifted(xp, 0, 1)
    s4 = s2 + shifted(s2, 1, 2)
    s8 = s4 + shifted(s4, 2, 4)
    s16 = s8 + shifted(s8, 3, 8)
    window = 2 << (_iota((1, POOL_WIDTH), 1) >> 6)
    count = jnp.minimum(t * tm + row + 1, window).astype(F32)
    pooled = _pool_select(s2, s4, s8, s16, count, xp)
    o_pool = _dot(pooled.astype(BF16), wpool_ref[...]) * pscale_ref[...]

    q = rest[:, R_Q:R_Q + 128]
    k = rest[:, R_K:R_K + 128]
    v = rest[:, R_V:R_V + GLA_WIDTH]
    la = rest[:, R_LOGA:R_LOGA + 128]
    r_i = _iota((CUMSUM_ROWS, CUMSUM_ROWS), 0)
    c_i = _iota((CUMSUM_ROWS, CUMSUM_ROWS), 1)
    same_chunk = (r_i >> 6) == (c_i >> 6)
    tri_bd = (same_chunk & (c_i <= r_i)).astype(BF16)
    ones_bd = same_chunk.astype(BF16)
    b, b_last = [], []
    for r0 in range(0, tm, CUMSUM_ROWS):
        la_parts = _split3(la[r0:r0 + CUMSUM_ROWS])
        b.append(_dot3(tri_bd, la_parts))
        b_last.append(_dot3(ones_bd, la_parts))
    b = jnp.concatenate(b, axis=0)
    b_last = jnp.concatenate(b_last, axis=0)
    qt = q * GLA_SCALE * jnp.exp(b)
    kt = k * jnp.exp(-b)
    kd = k * jnp.exp(b_last - b)
    dec = jnp.exp(b_last)
    head_mask = ((_iota((4 * GLA_CHUNK, 128), 0) >> 6) == (_iota((4 * GLA_CHUNK, 128), 1) >> 5)).astype(F32)
    causal = (_iota((4 * GLA_CHUNK, GLA_CHUNK), 0) & (GLA_CHUNK - 1)) >= _iota((4 * GLA_CHUNK, GLA_CHUNK), 1)
    lane_head = _iota((GLA_CHUNK, GLA_WIDTH), 1) >> 6
    for n in range(n_chunks):
        lo, hi = n * GLA_CHUNK, (n + 1) * GLA_CHUNK
        q_n = qt[lo:hi]
        v_n = v[lo:hi].astype(BF16)
        q_bd = (jnp.concatenate([q_n] * GLA_HEADS, axis=0) * head_mask).astype(BF16)
        a = _dot_nt(q_bd, kt[lo:hi].astype(BF16))
        a = jnp.where(causal, a, 0.0)
        pv = _dot(a.astype(BF16), v_n)
        o_n = jnp.zeros((GLA_CHUNK, GLA_WIDTH), F32)
        for hh in range(GLA_HEADS):
            o_n = o_n + jnp.where(lane_head == hh, pv[hh * GLA_CHUNK:(hh + 1) * GLA_CHUNK], 0.0)
        st = st_ref[...]
        o_n = o_n + _dot_nt(q_n.astype(BF16), st.astype(BF16))
        upd = _dot_tn(v_n, kd[lo:hi].astype(BF16))
        st_ref[...] = dec[lo:lo + 1, :] * st + upd * head_mask
        o_ref[lo:hi, :] = o_n
    o_gla = _head_rms(o_ref[...], ggla_ref[...])

    hout_ref[...] = _gate_and_project(omla_ref[...], rest, o_pool, o_gla, h_ref[...], wout_ref,
                                      gpost_ref[...])

    @pl.when(t == pl.num_programs(1) - 1)
    def _():
        sout_ref[...] = st_ref[...]


def _mix_prompt(rest, omla, h, w, *, batch, seq, tm):
    n = batch * seq
    nt = seq // tm
    full = lambda a: pl.BlockSpec(a.shape, lambda b, t, nd=a.ndim: (0,) * nd)
    row = lambda width: pl.BlockSpec((tm, width), lambda b, t: (b * nt + t, 0))
    weights = (w["w_pool"], w["pool_scale"], w["g_gla"], w["w_out"], w["g_post"])
    return pl.pallas_call(
        functools.partial(_mix_prompt_body, tm=tm),
        grid=(batch, nt),
        in_specs=[row(D_REST), row(512), row(D_MODEL)] + [full(a) for a in weights],
        out_specs=[row(D_MODEL), pl.BlockSpec((None, 4 * GLA_DV, 128), lambda b, t: (b, 0, 0))],
        out_shape=[jax.ShapeDtypeStruct((n, D_MODEL), F32),
                   jax.ShapeDtypeStruct((batch, 4 * GLA_DV, 128), F32)],
        scratch_shapes=[pltpu.VMEM((4, tm, POOL_WIDTH), F32), pltpu.VMEM((4 * GLA_DV, 128), F32),
                        pltpu.VMEM((tm, GLA_WIDTH), F32)],
        compiler_params=pltpu.CompilerParams(dimension_semantics=("parallel", "arbitrary"),
                                             vmem_limit_bytes=VMEM_LIMIT_BYTES),
        name="mix_prompt",
    )(rest, omla, h, *weights)


def _mix_sample_body(rest_ref, olat_ref, prefix_ref, s0_ref, h_ref, wuv_ref, wpool_ref, pscale_ref,
                     ggla_ref, wout_ref, gpost_ref,
                     hout_ref, snew_ref, st_ref, qT_ref, kT_ref, aT_ref, vT_ref, oT_ref, o_ref,
                     *, t_new, past_len):
    bs = LANES
    n_state_blocks = (GLA_HEADS * GLA_DK * GLA_DV) // LANES
    rest = rest_ref[...]

    o_mla = jnp.concatenate([_dot(olat_ref[h].astype(BF16), wuv_ref[h]) for h in range(MLA_HEADS)], axis=1)

    z = [prefix_ref[j] for j in range(POOL_BUF)]
    z += [rest[t * bs:(t + 1) * bs, R_POOL:R_POOL + POOL_WIDTH] for t in range(t_new)]
    pooled = []
    for t in range(t_new):
        e = POOL_BUF + t
        s2 = z[e] + z[e - 1]
        s4 = s2 + z[e - 2] + z[e - 3]
        s8 = s4 + z[e - 4] + z[e - 5] + z[e - 6] + z[e - 7]
        s16 = s8
        for i in range(8, 16):
            s16 = s16 + z[e - i]
        window = 2 << (_iota((1, POOL_WIDTH), 1) >> 6)
        count = jnp.minimum(past_len + t + 1, window).astype(F32)
        pooled.append(_pool_select(s2, s4, s8, s16, count, z[e]))
    pooled = jnp.concatenate(pooled, axis=0)
    o_pool = _dot(pooled.astype(BF16), wpool_ref[...]) * pscale_ref[...]

    for j in range(n_state_blocks):
        st_ref[j * LANES:(j + 1) * LANES, :] = s0_ref[:, j * LANES:(j + 1) * LANES].T
    for t in range(t_new):
        rows = slice(t * bs, (t + 1) * bs)
        qT_ref[t] = (rest[rows, R_Q:R_Q + 128] * GLA_SCALE).T
        kT_ref[t] = rest[rows, R_K:R_K + 128].T
        aT_ref[t] = jnp.exp(rest[rows, R_LOGA:R_LOGA + 128]).T
        vT_ref[t, 0:LANES] = rest[rows, R_V:R_V + LANES].T
        vT_ref[t, LANES:2 * LANES] = rest[rows, R_V + LANES:R_V + 2 * LANES].T
    for hh in range(GLA_HEADS):
        v_blocks = [vT_ref[t, hh * GLA_DV:(hh + 1) * GLA_DV, :] for t in range(t_new)]

        def body(d, carry, hh=hh, v_blocks=v_blocks):
            r = hh * GLA_DK + d
            base = pl.multiple_of(r * GLA_DV, GLA_DV)
            s = st_ref[pl.ds(base, GLA_DV), :]
            outs = []
            for t in range(t_new):
                s = aT_ref[t, pl.ds(r, 1), :] * s + kT_ref[t, pl.ds(r, 1), :] * v_blocks[t]
                outs.append(carry[t] + qT_ref[t, pl.ds(r, 1), :] * s)
            st_ref[pl.ds(base, GLA_DV), :] = s
            return tuple(outs)

        acc = lax.fori_loop(0, GLA_DK, body, tuple(jnp.zeros((GLA_DV, LANES), F32) for _ in range(t_new)))
        for t in range(t_new):
            oT_ref[t, hh * GLA_DV:(hh + 1) * GLA_DV, :] = acc[t]
    for j in range(n_state_blocks):
        snew_ref[:, j * LANES:(j + 1) * LANES] = st_ref[j * LANES:(j + 1) * LANES, :].T
    for t in range(t_new):
        o_ref[t * bs:(t + 1) * bs, 0:LANES] = oT_ref[t, 0:LANES, :].T
        o_ref[t * bs:(t + 1) * bs, LANES:2 * LANES] = oT_ref[t, LANES:2 * LANES, :].T
    o_gla = _head_rms(o_ref[...], ggla_ref[...])

    hout_ref[...] = _gate_and_project(o_mla, rest, o_pool, o_gla, h_ref[...], wout_ref, gpost_ref[...])


def _mix_sample(rest, olat, prefix_t, s0, h, w, *, t_new, past_len):
    n = rest.shape[0]
    bs = n // t_new
    assert bs == LANES, "sample mixing kernel keeps the sequences on the lane axis"
    state_w = GLA_HEADS * GLA_DK * GLA_DV
    args = (rest, olat, prefix_t, s0, h, w["w_uv"], w["w_pool"], w["pool_scale"], w["g_gla"], w["w_out"],
            w["g_post"])
    full = lambda a: pl.BlockSpec(a.shape, lambda i, nd=a.ndim: (0,) * nd)
    return pl.pallas_call(
        functools.partial(_mix_sample_body, t_new=t_new, past_len=past_len),
        grid=(1,),
        in_specs=[full(a) for a in args],
        out_specs=[pl.BlockSpec((n, D_MODEL), lambda i: (0, 0)), pl.BlockSpec((bs, state_w), lambda i: (0, 0))],
        out_shape=[jax.ShapeDtypeStruct((n, D_MODEL), F32), jax.ShapeDtypeStruct((bs, state_w), F32)],
        scratch_shapes=[
            pltpu.VMEM((state_w, LANES), F32),
            pltpu.VMEM((t_new, 128, LANES), F32), pltpu.VMEM((t_new, 128, LANES), F32),
            pltpu.VMEM((t_new, 128, LANES), F32), pltpu.VMEM((t_new, GLA_WIDTH, LANES), F32),
            pltpu.VMEM((t_new, GLA_WIDTH, LANES), F32), pltpu.VMEM((n, GLA_WIDTH), F32),
        ],
        compiler_params=pltpu.CompilerParams(dimension_semantics=("arbitrary",),
                                             vmem_limit_bytes=VMEM_LIMIT_BYTES),
        name="mix_sample",
    )(*args)


def _rope_table(pos):
    half = MLA_ROPE // 2
    inv = 1.0 / (ROPE_BASE ** (jnp.arange(half, dtype=F32) / half))
    ang = pos.astype(F32)[:, None] * inv[None, :]
    cos, sin = jnp.cos(ang), jnp.sin(ang)
    return jnp.concatenate([cos, cos, -sin, sin], axis=1)


def _layer_weights(l, g_pre, w_in, g_cq, w_uq, g_ckv, w_uk, w_uv, w_pool, pool_scale, w_a2, b_a, g_gla,
                   w_out, g_post):
    half = MLA_ROPE // 2
    wi = w_in[l]
    w_in_ext = jnp.concatenate([
        wi[:, 0:576],
        wi[:, 544:576], wi[:, 512:544],
        wi[:, 576:2112],
        wi[:, 2128:2384],
        wi[:, 2112:2128],
        jnp.zeros((D_MODEL, LANES - GLA_GATE_RANK), F32),
    ], axis=1).astype(BF16)
    wq = w_uq[l]
    rope_cols = wq[:, :, MLA_NOPE:]
    w_uq_ext = jnp.concatenate([wq, rope_cols[:, :, half:], rope_cols[:, :, :half]], axis=2)
    w_uq_ext = w_uq_ext.reshape(Q_RANK, MLA_HEADS * 256).astype(BF16)
    w_pool_bd = jnp.zeros((POOL_WIDTH, POOL_WIDTH), F32)
    for g in range(len(POOL_WINDOWS)):
        w_pool_bd = w_pool_bd.at[g * POOL_GC:(g + 1) * POOL_GC, g * POOL_GC:(g + 1) * POOL_GC].set(w_pool[l, g])
    w_a2_pad = jnp.zeros((LANES, GLA_HEADS * GLA_DK), F32).at[:GLA_GATE_RANK].set(w_a2[l])
    return {
        "g_pre": g_pre[l][None, :], "w_in": w_in_ext, "g_cq": g_cq[l][None, :], "w_uq": w_uq_ext,
        "g_ckv": g_ckv[l][None, :],
        "w_uk": jnp.transpose(w_uk[l], (1, 2, 0)).astype(BF16),
        "w_uv": jnp.transpose(w_uv[l], (1, 0, 2)).astype(BF16),
        "w_pool": w_pool_bd.astype(BF16), "pool_scale": pool_scale[l][None, :],
        "w_a2": w_a2_pad.astype(BF16), "b_a": b_a[l][None, :], "g_gla": g_gla[l][None, :],
        "w_out": w_out[l].astype(BF16), "g_post": g_post[l][None, :],
    }


def _tile_sizes(seq):
    pick = lambda want: max(t for t in (256, 512) if t <= want and seq % t == 0)
    return pick(512), pick(512), pick(512)


@jax.jit
def _forward(x_prompt, x_sample, cache_ckv, cache_krope, state_pool, state_gla, page_table,
             g_pre, w_in, g_cq, w_uq, g_ckv, w_uk, w_uv, w_pool, pool_scale, w_a2, b_a, g_gla, w_out, g_post):
    bp, tp, _ = x_prompt.shape
    bs, ts, _ = x_sample.shape
    depth = w_in.shape[0]
    n_pages = page_table.shape[1]
    past_len = n_pages * PAGE_SIZE
    tm_in, tq, tm_mix = _tile_sizes(tp)
    assert n_pages % (2 * ROPE_SUB) == 0 and MLA_HEADS * ts == 16
    pages_per_step = max(p for p in (8, 16, 32) if n_pages % (2 * p) == 0)
    cache_krope_t = jnp.swapaxes(cache_krope, 2, 3)

    tab_p = _rope_table(jnp.arange(tp))
    tab_s = jnp.repeat(_rope_table(past_len + jnp.arange(ts)), bs, axis=0)
    pt_flat = page_table.reshape(-1).astype(jnp.int32)

    hp = x_prompt.reshape(bp * tp, D_MODEL)
    hs = jnp.transpose(x_sample, (1, 0, 2)).reshape(ts * bs, D_MODEL)
    outs = {k: [] for k in ("ckv_p", "kr_p", "pool_p", "gla_p", "ckv_s", "kr_s", "pool_s", "gla_s")}
    for l in range(depth):
        w = _layer_weights(l, g_pre, w_in, g_cq, w_uq, g_ckv, w_uk, w_uv, w_pool, pool_scale, w_a2, b_a,
                           g_gla, w_out, g_post)
        qc, ckv, kr, kc, rest = _inproj(hp, tab_p, w, tm=tm_in)
        omla = _attn_prompt(qc, kc, w["w_uv"], batch=bp, seq=tp, tq=tq)
        hp, st = _mix_prompt(rest, omla, hp, w, batch=bp, seq=tp, tm=tm_mix)
        outs["ckv_p"].append(ckv.reshape(bp, tp, KV_RANK))
        outs["kr_p"].append(kr.reshape(bp, tp, MLA_ROPE))
        outs["pool_p"].append(rest.reshape(bp, tp, D_REST)[:, tp - POOL_BUF:, R_POOL:R_POOL + POOL_WIDTH])
        st = st.reshape(bp, GLA_HEADS, GLA_DV, GLA_HEADS, GLA_DK)
        st = jnp.stack([st[:, h, :, h, :] for h in range(GLA_HEADS)], axis=1)
        outs["gla_p"].append(jnp.swapaxes(st, 2, 3))
        qc, ckv, kr, _, rest = _inproj(hs, tab_s, w, tm=ts * bs)
        q_b = jnp.transpose(qc.reshape(MLA_HEADS, ts, bs, D_QK), (2, 0, 1, 3)).reshape(bs, MLA_HEADS * ts, D_QK)
        ckv_b = jnp.transpose(ckv.reshape(ts, bs, KV_RANK), (1, 0, 2))
        kr_b = jnp.transpose(kr.reshape(ts, bs, MLA_ROPE), (1, 0, 2))
        olat = _attn_sample(pt_flat, q_b, ckv_b, kr_b, cache_ckv, cache_krope_t, layer=l, n_pages=n_pages,
                            pages=pages_per_step)
        olat = jnp.transpose(olat.reshape(bs, MLA_HEADS, ts, KV_RANK), (1, 2, 0, 3)).reshape(
            MLA_HEADS, ts * bs, KV_RANK)
        prefix_t = jnp.transpose(state_pool[l], (1, 0, 2))
        s0 = state_gla[l].reshape(bs, GLA_HEADS * GLA_DK * GLA_DV)
        hs, s_new = _mix_sample(rest, olat, prefix_t, s0, hs, w, t_new=ts, past_len=past_len)
        xp_b = jnp.transpose(rest[:, R_POOL:R_POOL + POOL_WIDTH].reshape(ts, bs, POOL_WIDTH), (1, 0, 2))
        outs["ckv_s"].append(ckv_b)
        outs["kr_s"].append(kr_b)
        outs["pool_s"].append(jnp.concatenate([state_pool[l], xp_b], axis=1)[:, ts:])
        outs["gla_s"].append(s_new.reshape(bs, GLA_HEADS, GLA_DK, GLA_DV))
    y_prompt = hp.reshape(bp, tp, D_MODEL)
    y_sample = jnp.transpose(hs.reshape(ts, bs, D_MODEL), (1, 0, 2))
    return (y_prompt, y_sample,
            jnp.stack(outs["ckv_p"]), jnp.stack(outs["kr_p"]), jnp.stack(outs["pool_p"]), jnp.stack(outs["gla_p"]),
            jnp.stack(outs["ckv_s"]), jnp.stack(outs["kr_s"]), jnp.stack(outs["pool_s"]), jnp.stack(outs["gla_s"]))


def kernel(x_prompt, x_sample, cache_ckv, cache_krope, state_pool, state_gla, page_table, g_pre, w_in, g_cq,
           w_uq, g_ckv, w_uk, w_uv, w_pool, pool_scale, w_a2, b_a, g_gla, w_out, g_post):
    return _forward(x_prompt, x_sample, cache_ckv, cache_krope, state_pool, state_gla, page_table, g_pre, w_in,
                    g_cq, w_uq, g_ckv, w_uk, w_uv, w_pool, pool_scale, w_a2, b_a, g_gla, w_out, g_post)
```
